```python
import jax, jax.numpy as jnp
from jax import lax
import numpy as np

D_MODEL = 1024
BATCH = 32
SEQ = 2048
DEPTH = 1

GRID_W = 64
CTX_LEN = 256
D_FF = 2816
D_CONV = 1024
CONV_WIDTH = 31
GLA_HEADS = 4
GLA_DK = 128
GLA_DV = 256
GLA_LOWRANK = 16
GLA_TAU = 16.0
GLA_CHUNK = 64
N_MOD = 9
EPS = 1e-6
QK_W = GLA_HEADS * GLA_DK
V_W = GLA_HEADS * GLA_DV
IN_SPLITS = (2 * D_CONV, QK_W, QK_W, V_W, V_W, GLA_LOWRANK, GLA_LOWRANK, D_MODEL, D_MODEL)
D_IN = 2 * D_CONV + 2 * QK_W + 2 * V_W + 2 * GLA_LOWRANK + 2 * D_MODEL

kernel_name = 'hybrid_conv_gla_macaron_dit_layer'


def rmsnorm(h, g):
    hf = h.astype(jnp.float32)
    y = hf * lax.rsqrt(jnp.mean(hf * hf, axis=-1, keepdims=True) + EPS)
    return (y * g.astype(jnp.float32)).astype(h.dtype)


def layernorm(h, g, b):
    hf = h.astype(jnp.float32)
    mu = jnp.mean(hf, axis=-1, keepdims=True)
    var = jnp.mean(jnp.square(hf - mu), axis=-1, keepdims=True)
    y = (hf - mu) * lax.rsqrt(var + EPS)
    return (y * g.astype(jnp.float32) + b.astype(jnp.float32)).astype(h.dtype)


def modulate(h, g, shift, scale):
    return rmsnorm(h, g) * (1 + scale) + shift


def half_ffn(h, g, shift, scale, gate, w_gu, w_down):
    u = modulate(h, g, shift, scale)
    a, b = jnp.split(u @ w_gu, 2, axis=-1)
    return h + 0.5 * gate * ((jax.nn.silu(a) * b) @ w_down)


def split_in(p):
    offs, o = [], 0
    for w in IN_SPLITS[:-1]:
        o += w
        offs.append(o)
    return jnp.split(p, offs, axis=-1)


def heads(t, d):
    return t.reshape(t.shape[:-1] + (GLA_HEADS, d))


def flip(t):
    return jnp.flip(t, axis=1)


def log_decay(lr, w, b):
    z = (lr @ w + b).astype(jnp.float32)
    return heads(jax.nn.log_sigmoid(z) / GLA_TAU, GLA_DK)


def conformer_conv(p, dw_w, dw_b, ln_g, ln_b, w_o):
    a, b = jnp.split(p, 2, axis=-1)
    z = a * jax.nn.sigmoid(b)
    z = lax.conv_general_dilated(
        z, dw_w[:, None, :].astype(z.dtype), window_strides=(1,),
        padding=((CONV_WIDTH // 2, CONV_WIDTH // 2),),
        dimension_numbers=('NWC', 'WIO', 'NWC'), feature_group_count=D_CONV) + dw_b
    z = jax.nn.silu(layernorm(z, ln_g, ln_b))
    return z @ w_o


def gla_scan(q, k, v, log_a, s0):
    bsz, L = q.shape[0], q.shape[1]
    n = L // GLA_CHUNK

    def chunks(t):
        t = t.astype(jnp.float32).reshape(bsz, n, GLA_CHUNK, GLA_HEADS, t.shape[-1])
        return jnp.transpose(t, (1, 0, 3, 2, 4))

    mask = jnp.tril(jnp.ones((GLA_CHUNK, GLA_CHUNK), dtype=bool))[:, :, None]

    def step(s, inp):
        qc, kc, vc, gc = inp
        b = jnp.cumsum(gc, axis=2)
        inter = jnp.einsum('bhtk,bhkv->bhtv', qc * jnp.exp(b), s)
        rel = jnp.where(mask, b[:, :, :, None, :] - b[:, :, None, :, :], -jnp.inf)
        att = jnp.einsum('bhtk,bhtsk,bhsk->bhts', qc, jnp.exp(rel), kc)
        intra = jnp.einsum('bhts,bhsv->bhtv', att, vc)
        b_last = b[:, :, -1:, :]
        s_new = jnp.exp(b_last[:, :, 0, :, None]) * s + jnp.einsum(
            'bhsk,bhsv->bhkv', kc * jnp.exp(b_last - b), vc)
        return s_new, inter + intra

    _, o = lax.scan(step, s0, (chunks(q), chunks(k), chunks(v), chunks(log_a)))
    return jnp.transpose(o, (1, 0, 3, 2, 4)).reshape(bsz, L, GLA_HEADS, GLA_DV)


def gla_final_state(k, v, log_a):
    b = jnp.cumsum(log_a.astype(jnp.float32), axis=1)
    w = jnp.exp(b[:, -1:] - b)
    return jnp.einsum('blhk,blhv->bhkv', k.astype(jnp.float32) * w, v.astype(jnp.float32))


def bidir_gla(q, k, v, la_f, la_b, s_f, s_b):
    o_f = gla_scan(q, k, v, la_f, s_f)
    o_b = gla_scan(flip(q), flip(k), flip(v), flip(la_b), s_b)
    return o_f + flip(o_b)


def gla_output(o, og, gn_g, w_go):
    o = o * lax.rsqrt(jnp.mean(o * o, axis=-1, keepdims=True) + EPS)
    o = (o.reshape(o.shape[0], o.shape[1], V_W) * gn_g.astype(jnp.float32)).astype(og.dtype)
    return (o * jax.nn.silu(og)) @ w_go


def branch_merge(conv_in, o, og, ga, gb, dw_w, dw_b, ln_g, ln_b, w_co, gn_g, w_go, w_o):
    y_conv = conformer_conv(conv_in, dw_w, dw_b, ln_g, ln_b, w_co)
    y_gla = gla_output(o, og, gn_g, w_go)
    return (jax.nn.sigmoid(ga) * y_conv + jax.nn.sigmoid(gb) * y_gla) @ w_o


def setup_inputs(seed: int = 0) -> dict:
    key = jax.random.key(seed)
    ks = jax.random.split(key, 32)
    f32 = jnp.float32
    D, L = D_MODEL, DEPTH

    def nrm(k, shape, scale):
        return jax.random.normal(k, shape, f32) * scale

    def gain(k, shape):
        return 1.0 + 0.05 * jax.random.normal(k, shape, f32)

    return {
        'x': nrm(ks[0], (BATCH, SEQ, D), 1.0),
        'c': nrm(ks[1], (BATCH, D), 1.0),
        'ctx': nrm(ks[2], (BATCH, CTX_LEN, D), 1.0),
        'c_ctx': nrm(ks[3], (D,), 1.0),
        'w_mod': nrm(ks[4], (L, D, N_MOD * D), 0.5 * D ** -0.5),
        'b_mod': nrm(ks[5], (L, N_MOD * D), 0.01),
        'g_ffn1': gain(ks[6], (L, D)),
        'w1_gu': nrm(ks[7], (L, D, 2 * D_FF), D ** -0.5),
        'w1_down': nrm(ks[8], (L, D_FF, D), D_FF ** -0.5),
        'g_mix': gain(ks[9], (L, D)),
        'w_in': nrm(ks[10], (L, D, D_IN), D ** -0.5),
        'dw_weight': nrm(ks[11], (L, CONV_WIDTH, D_CONV), CONV_WIDTH ** -0.5),
        'dw_bias': nrm(ks[12], (L, D_CONV), 0.01),
        'conv_ln_g': gain(ks[13], (L, D_CONV)),
        'conv_ln_b': nrm(ks[14], (L, D_CONV), 0.01),
        'w_conv_out': nrm(ks[15], (L, D_CONV, D), D_CONV ** -0.5),
        'w_alpha_f': nrm(ks[16], (L, GLA_LOWRANK, QK_W), GLA_LOWRANK ** -0.5),
        'b_alpha_f': nrm(ks[17], (L, QK_W), 0.1),
        'w_alpha_b': nrm(ks[18], (L, GLA_LOWRANK, QK_W), GLA_LOWRANK ** -0.5),
        'b_alpha_b': nrm(ks[19], (L, QK_W), 0.1),
        'gla_norm_g': gain(ks[20], (L, V_W)),
        'w_gla_out': nrm(ks[21], (L, V_W, D), V_W ** -0.5),
        'w_out': nrm(ks[22], (L, D, D), D ** -0.5),
        'g_ffn2': gain(ks[23], (L, D)),
        'w2_gu': nrm(ks[24], (L, D, 2 * D_FF), D ** -0.5),
        'w2_down': nrm(ks[25], (L, D_FF, D), D_FF ** -0.5),
        'g_final': gain(ks[26], (D,)),
    }


def reference(x, c, ctx, c_ctx, w_mod, b_mod, g_ffn1, w1_gu, w1_down, g_mix, w_in,
              dw_weight, dw_bias, conv_ln_g, conv_ln_b, w_conv_out, w_alpha_f, b_alpha_f,
              w_alpha_b, b_alpha_b, gla_norm_g, w_gla_out, w_out, g_ffn2, w2_gu, w2_down, g_final):
    q_scale = GLA_DK ** -0.5
    h = ctx
    for l in range(DEPTH):
        last = l == DEPTH - 1
        mx = jnp.split((jax.nn.silu(c) @ w_mod[l] + b_mod[l])[:, None, :], N_MOD, axis=-1)
        mc = jnp.split(jax.nn.silu(c_ctx) @ w_mod[l] + b_mod[l], N_MOD, axis=-1)

        x = half_ffn(x, g_ffn1[l], mx[0], mx[1], mx[2], w1_gu[l], w1_down[l])
        h = half_ffn(h, g_ffn1[l], mc[0], mc[1], mc[2], w1_gu[l], w1_down[l])

        conv_x, q_x, k_x, v_x, og_x, af_x, ab_x, ga_x, gb_x = split_in(
            modulate(x, g_mix[l], mx[3], mx[4]) @ w_in[l])
        conv_c, q_c, k_c, v_c, og_c, af_c, ab_c, ga_c, gb_c = split_in(
            modulate(h, g_mix[l], mc[3], mc[4]) @ w_in[l])

        k_c, v_c = heads(k_c, GLA_DK), heads(v_c, GLA_DV)
        laf_c = log_decay(af_c, w_alpha_f[l], b_alpha_f[l])
        lab_c = log_decay(ab_c, w_alpha_b[l], b_alpha_b[l])
        s_f = gla_final_state(k_c, v_c, laf_c)
        s_b = gla_final_state(flip(k_c), flip(v_c), flip(lab_c))

        o_x = bidir_gla(heads(q_x, GLA_DK) * q_scale, heads(k_x, GLA_DK), heads(v_x, GLA_DV),
                        log_decay(af_x, w_alpha_f[l], b_alpha_f[l]),
                        log_decay(ab_x, w_alpha_b[l], b_alpha_b[l]), s_f, s_b)
        mix_x = branch_merge(conv_x, o_x, og_x, ga_x, gb_x, dw_weight[l], dw_bias[l], conv_ln_g[l],
                             conv_ln_b[l], w_conv_out[l], gla_norm_g[l], w_gla_out[l], w_out[l])

        if not last:
            zeros = jnp.zeros_like(s_f)
            o_c = bidir_gla(heads(q_c, GLA_DK) * q_scale, k_c, v_c, laf_c, lab_c, zeros, zeros)
            mix_c = branch_merge(conv_c, o_c, og_c, ga_c, gb_c, dw_weight[l], dw_bias[l], conv_ln_g[l],
                                 conv_ln_b[l], w_conv_out[l], gla_norm_g[l], w_gla_out[l], w_out[l])
            h = h + mc[5] * mix_c
            h = half_ffn(h, g_ffn2[l], mc[6], mc[7], mc[8], w2_gu[l], w2_down[l])

        x = x + mx[5] * mix_x
        x = half_ffn(x, g_ffn2[l], mx[6], mx[7], mx[8], w2_gu[l], w2_down[l])
    return rmsnorm(x, g_final)
```

```python
import functools

import jax
import jax.numpy as jnp
from jax import lax
from jax.experimental import pallas as pl
from jax.experimental.pallas import tpu as pltpu

F32 = jnp.float32
BF16 = jnp.bfloat16

EPS = 1e-6
N_MOD = 9
GLA_HEADS = 4
GLA_TAU = 16.0
GLA_LOWRANK = 16
GLA_CHUNK = 128
LANE = 128
VMEM_LIMIT = 56 * 1024 * 1024


def _dot(a, b):
    return jnp.dot(a, b, preferred_element_type=F32)


def _silu(x):
    return x * jax.nn.sigmoid(x)


def _rms(h, g):
    return h * lax.rsqrt(jnp.mean(h * h, axis=-1, keepdims=True) + EPS) * g


def _modulate(h, g, mod_ref, row):
    shift = mod_ref[0, row:row + 1, :]
    scale = mod_ref[0, row + 1:row + 2, :]
    return _rms(h, g) * (1.0 + scale) + shift


def _const_spec(shape):
    nd = len(shape)
    return pl.BlockSpec(shape, lambda *_: (0,) * nd, pipeline_mode=pl.Buffered(1))


def _params(sem):
    return pltpu.CompilerParams(dimension_semantics=sem, vmem_limit_bytes=VMEM_LIMIT)


def _mod_kernel(c_ref, w_ref, b_ref, o_ref):
    s = _silu(c_ref[...]).astype(BF16)
    o_ref[...] = _dot(s, w_ref[...].astype(BF16)) + b_ref[...]


def _mod_call(c_all, w_mod, b_mod, n_steps=8):
    rows, d = c_all.shape
    n = w_mod.shape[1]
    bn = n // n_steps
    return pl.pallas_call(
        _mod_kernel,
        grid=(n_steps,),
        in_specs=[pl.BlockSpec((rows, d), lambda j: (0, 0)),
                  pl.BlockSpec((d, bn), lambda j: (0, j)),
                  pl.BlockSpec((1, bn), lambda j: (0, j))],
        out_specs=pl.BlockSpec((rows, bn), lambda j: (0, j)),
        out_shape=jax.ShapeDtypeStruct((rows, n), F32),
        compiler_params=_params(("arbitrary",)),
        name="mod",
    )(c_all, w_mod, b_mod.reshape(1, n))


def _ffn_kernel(h_ref, mod_ref, g_ref, wgu_ref, wd_ref, *rest, mod_row, n_chunks, final_norm):
    if final_norm:
        gfin_ref, o_ref = rest
    else:
        (o_ref,) = rest
    h = h_ref[0]
    u = _modulate(h, g_ref[...], mod_ref, mod_row).astype(BF16)
    f = wd_ref.shape[0]
    fc = f // n_chunks
    acc = None
    for j in range(n_chunks):
        a = _dot(u, wgu_ref[:, j * fc:(j + 1) * fc])
        b = _dot(u, wgu_ref[:, f + j * fc:f + (j + 1) * fc])
        m = (_silu(a) * b).astype(BF16)
        d = _dot(m, wd_ref[j * fc:(j + 1) * fc, :])
        acc = d if acc is None else acc + d
    gate = mod_ref[0, mod_row + 2:mod_row + 3, :]
    y = h + (0.5 * gate) * acc
    if final_norm:
        y = _rms(y, gfin_ref[...])
    o_ref[0] = y


def _ffn_call(h, mod, mod_per_batch, g, wgu, wd, mod_row, tm, g_final=None, name="ffn"):
    bsz, s, d = h.shape
    f = wd.shape[0]
    mod_idx = (lambda b, j: (b, 0, 0)) if mod_per_batch else (lambda b, j: (0, 0, 0))
    in_specs = [pl.BlockSpec((1, tm, d), lambda b, j: (b, j, 0)),
                pl.BlockSpec((1, N_MOD, d), mod_idx),
                _const_spec((1, d)),
                _const_spec((d, 2 * f)),
                _const_spec((f, d))]
    args = [h, mod, g.reshape(1, d), wgu, wd]
    if g_final is not None:
        in_specs.append(_const_spec((1, d)))
        args.append(g_final.reshape(1, d))
    kern = functools.partial(_ffn_kernel, mod_row=mod_row, n_chunks=2, final_norm=g_final is not None)
    return pl.pallas_call(
        kern,
        grid=(bsz, s // tm),
        in_specs=in_specs,
        out_specs=pl.BlockSpec((1, tm, d), lambda b, j: (b, j, 0)),
        out_shape=jax.ShapeDtypeStruct((bsz, s, d), F32),
        compiler_params=_params(("parallel", "parallel")),
        name=name,
    )(*args)


def _inproj_kernel(x_ref, mod_ref, g_ref, w_ref, wlr_ref, wal_ref, bal_ref, *out_refs, segs):
    u = _modulate(x_ref[0], g_ref[...], mod_ref, 3).astype(BF16)
    off = 0
    for (kind, width), o_ref in zip(segs, out_refs[:-1]):
        if kind == "glu":
            a = _dot(u, w_ref[:, off:off + width])
            b = _dot(u, w_ref[:, off + width:off + 2 * width])
            o_ref[0] = (a * jax.nn.sigmoid(b)).astype(o_ref.dtype)
            off += 2 * width
        else:
            p = _dot(u, w_ref[:, off:off + width])
            if kind == "sigmoid":
                p = jax.nn.sigmoid(p)
            o_ref[0] = p.astype(o_ref.dtype)
            off += width
    lr = _dot(u, wlr_ref[...]).astype(BF16)
    z = _dot(lr, wal_ref[...]) + bal_ref[...]
    log_sig = jnp.minimum(z, 0.0) - jnp.log1p(jnp.exp(-jnp.abs(z)))
    out_refs[-1][0] = log_sig * (1.0 / GLA_TAU)


def _inproj_call(x, mod, mod_per_batch, g, w, wlr, wal, bal, segs, tm, name):
    bsz, s, d = x.shape
    mod_idx = (lambda b, j: (b, 0, 0)) if mod_per_batch else (lambda b, j: (0, 0, 0))
    n_dec = wal.shape[1]
    tok = lambda n: pl.BlockSpec((1, tm, n), lambda b, j: (b, j, 0))
    out_shape = [jax.ShapeDtypeStruct((bsz, s, width), BF16) for _, width in segs]
    out_shape.append(jax.ShapeDtypeStruct((bsz, s, n_dec), F32))
    out_specs = [tok(width) for _, width in segs] + [tok(n_dec)]
    return pl.pallas_call(
        functools.partial(_inproj_kernel, segs=segs),
        grid=(bsz, s // tm),
        in_specs=[tok(d),
                  pl.BlockSpec((1, N_MOD, d), mod_idx),
                  _const_spec((1, d)),
                  _const_spec(w.shape),
                  _const_spec(wlr.shape),
                  _const_spec(wal.shape),
                  _const_spec((1, n_dec))],
        out_specs=out_specs,
        out_shape=out_shape,
        compiler_params=_params(("parallel", "parallel")),
        name=name,
    )(x, mod, g.reshape(1, d), w, wlr, wal, bal.reshape(1, n_dec))


def _split3(g):
    hi = g.astype(BF16)
    r1 = g - hi.astype(F32)
    mid = r1.astype(BF16)
    lo = (r1 - mid.astype(F32)).astype(BF16)
    return hi, mid, lo


def _cumsum(tri, g):
    hi, mid, lo = _split3(g)
    return _dot(tri, hi) + _dot(tri, mid) + _dot(tri, lo)


def _dot_t(a, b, ca, cb):
    return lax.dot_general(a, b, (((ca,), (cb,)), ((), ())), preferred_element_type=F32)


def _state_step(st, k, v, g, tri, last):
    b = _cumsum(tri, g)
    bl = b[last:last + 1, :]
    kl = (k.astype(F32) * jnp.exp(bl - b)).astype(BF16)
    return st * jnp.exp(bl) + _dot_t(v, kl, 0, 0)


def _chunk_step(st, q, k, v, g, tri, mask, last, mid, scale):
    b = _cumsum(tri, g)
    bl = b[last:last + 1, :]
    bm = b[mid:mid + 1, :]
    qm = q.astype(F32) * scale * jnp.exp(b - bm)
    qb = (qm * jnp.exp(bm)).astype(BF16)
    km = k.astype(F32) * jnp.exp(bm - b)
    kl = (km * jnp.exp(bl - bm)).astype(BF16)
    att = _dot_t(qm.astype(BF16), km.astype(BF16), 1, 1)
    att = jnp.where(mask, att, 0.0).astype(BF16)
    o = _dot_t(qb, st.astype(BF16), 1, 1) + _dot(att, v)
    st_new = st * jnp.exp(bl) + _dot_t(v, kl, 0, 0)
    return o, st_new


def _gla_kernel(q_ref, k_ref, v_ref, gf_ref, gb_ref, kc_ref, vc_ref, gfc_ref, gbc_ref,
                o_ref, sf_ref, sb_ref, *, chunk, scale):
    c = chunk
    n = q_ref.shape[1] // c
    nc = kc_ref.shape[1] // c
    row = lax.broadcasted_iota(jnp.int32, (c, c), 0)
    col = lax.broadcasted_iota(jnp.int32, (c, c), 1)
    mask_f = row >= col
    mask_b = row <= col
    tri_f = mask_f.astype(BF16)
    tri_b = mask_b.astype(BF16)

    sf = jnp.zeros(sf_ref.shape, F32)
    sb = jnp.zeros(sb_ref.shape, F32)
    for i in range(nc):
        lo, rlo = i * c, (nc - 1 - i) * c
        sf = _state_step(sf, kc_ref[0, lo:lo + c, :], vc_ref[0, lo:lo + c, :],
                         gfc_ref[0, lo:lo + c, :], tri_f, c - 1)
        sb = _state_step(sb, kc_ref[0, rlo:rlo + c, :], vc_ref[0, rlo:rlo + c, :],
                         gbc_ref[0, rlo:rlo + c, :], tri_b, 0)
    sf_ref[...] = sf
    sb_ref[...] = sb

    def run(i, accumulate):
        lo = pl.multiple_of(i * c, c)
        rlo = pl.multiple_of((n - 1 - i) * c, c)
        sl, rsl = pl.ds(lo, c), pl.ds(rlo, c)
        of, sf_new = _chunk_step(sf_ref[...], q_ref[0, sl, :], k_ref[0, sl, :], v_ref[0, sl, :],
                                 gf_ref[0, sl, :], tri_f, mask_f, c - 1, c // 2, scale)
        ob, sb_new = _chunk_step(sb_ref[...], q_ref[0, rsl, :], k_ref[0, rsl, :], v_ref[0, rsl, :],
                                 gb_ref[0, rsl, :], tri_b, mask_b, 0, c // 2, scale)
        sf_ref[...] = sf_new
        sb_ref[...] = sb_new
        if accumulate:
            o_ref[0, sl, :] += of
            o_ref[0, rsl, :] += ob
        else:
            o_ref[0, sl, :] = of
            o_ref[0, rsl, :] = ob

    def first(i, carry):
        run(i, False)
        return carry

    def second(i, carry):
        run(i, True)
        return carry

    lax.fori_loop(0, n // 2, first, 0)
    lax.fori_loop(n // 2, n, second, 0)


def _gla_call(q, k, v, gd, kc, vc, gdc, chunk):
    bsz, s, qk_w = q.shape
    v_w = v.shape[2]
    sc = kc.shape[1]
    dk, dv = qk_w // GLA_HEADS, v_w // GLA_HEADS
    assert s % (2 * chunk) == 0 and sc % chunk == 0 and dk == LANE
    hd = lambda n, w, shift=0: pl.BlockSpec((1, n, w), lambda b, h: (b, 0, h + shift))
    return pl.pallas_call(
        functools.partial(_gla_kernel, chunk=chunk, scale=dk ** -0.5),
        grid=(bsz, GLA_HEADS),
        in_specs=[hd(s, dk), hd(s, dk), hd(s, dv), hd(s, dk), hd(s, dk, GLA_HEADS),
                  hd(sc, dk), hd(sc, dv), hd(sc, dk), hd(sc, dk, GLA_HEADS)],
        out_specs=hd(s, dv),
        out_shape=jax.ShapeDtypeStruct((bsz, s, v_w), F32),
        scratch_shapes=[pltpu.VMEM((dv, dk), F32), pltpu.VMEM((dv, dk), F32)],
        compiler_params=_params(("parallel", "parallel")),
        name="gla",
    )(q, k, v, gd, gd, kc, vc, gdc, gdc)


def _merge_kernel(zc_ref, o_ref, og_ref, sga_ref, sgb_ref, x_ref, mod_ref, dww_ref, dwb_ref,
                  lng_ref, lnb_ref, gng_ref, wco_ref, wgo_ref, wo_ref, out_ref, win_ref, *, halo):
    tm = x_ref.shape[1]
    seq = zc_ref.shape[1]
    width = dww_ref.shape[0]
    pad = width // 2
    j = pl.program_id(1)
    nj = pl.num_programs(1)
    t0 = pl.multiple_of(j * tm, tm)

    prev_lo = pl.multiple_of(jnp.maximum(t0 - halo, 0), halo)
    next_lo = pl.multiple_of(jnp.minimum(t0 + tm, seq - halo), halo)
    has_prev = (j > 0).astype(F32)
    has_next = (j < nj - 1).astype(F32)
    win_ref[0:halo, :] = zc_ref[0, pl.ds(prev_lo, halo), :].astype(F32) * has_prev
    win_ref[halo:halo + tm, :] = zc_ref[0, pl.ds(t0, tm), :].astype(F32)
    win_ref[halo + tm:, :] = zc_ref[0, pl.ds(next_lo, halo), :].astype(F32) * has_next

    acc = None
    for d in range(width):
        lo = halo - pad + d
        term = win_ref[lo:lo + tm, :] * dww_ref[d:d + 1, :]
        acc = term if acc is None else acc + term
    z = acc + dwb_ref[...]
    mu = jnp.mean(z, axis=-1, keepdims=True)
    zc = z - mu
    var = jnp.mean(zc * zc, axis=-1, keepdims=True)
    z = zc * lax.rsqrt(var + EPS) * lng_ref[...] + lnb_ref[...]
    y_conv = _dot(_silu(z).astype(BF16), wco_ref[...])

    o = o_ref[0]
    dv = o.shape[1] // GLA_HEADS
    parts = []
    for h in range(GLA_HEADS):
        oh = o[:, h * dv:(h + 1) * dv]
        parts.append(oh * lax.rsqrt(jnp.mean(oh * oh, axis=-1, keepdims=True) + EPS))
    on = jnp.concatenate(parts, axis=1) * gng_ref[...]
    y_gla = _dot((on * _silu(og_ref[0].astype(F32))).astype(BF16), wgo_ref[...])

    merged = sga_ref[0].astype(F32) * y_conv + sgb_ref[0].astype(F32) * y_gla
    mix = _dot(merged.astype(BF16), wo_ref[...])
    out_ref[0] = x_ref[0] + mod_ref[0, 5:6, :] * mix


def _merge_call(zc, o, og, sga, sgb, x1, mod, dw_w, dw_b, ln_g, ln_b, gn_g, wco, wgo, wo, tm):
    bsz, s, d = x1.shape
    dc = zc.shape[2]
    halo = 16
    assert dw_w.shape[0] // 2 <= halo and s % tm == 0
    tok = lambda n: pl.BlockSpec((1, tm, n), lambda b, j: (b, j, 0))
    row = lambda a: a.reshape(1, -1)
    return pl.pallas_call(
        functools.partial(_merge_kernel, halo=halo),
        grid=(bsz, s // tm),
        in_specs=[pl.BlockSpec((1, s, dc), lambda b, j: (b, 0, 0)),
                  tok(o.shape[2]), tok(og.shape[2]), tok(d), tok(d), tok(d),
                  pl.BlockSpec((1, N_MOD, d), lambda b, j: (b, 0, 0)),
                  _const_spec(dw_w.shape), _const_spec((1, dc)), _const_spec((1, dc)), _const_spec((1, dc)),
                  _const_spec((1, o.shape[2])),
                  _const_spec(wco.shape), _const_spec(wgo.shape), _const_spec(wo.shape)],
        out_specs=tok(d),
        out_shape=jax.ShapeDtypeStruct((bsz, s, d), F32),
        scratch_shapes=[pltpu.VMEM((tm + 2 * halo, dc), F32)],
        compiler_params=_params(("parallel", "arbitrary")),
        name="merge",
    )(zc, o, og, sga, sgb, x1, mod, dw_w, row(dw_b), row(ln_g), row(ln_b), row(gn_g), wco, wgo, wo)


def kernel(x, c, ctx, c_ctx, w_mod, b_mod, g_ffn1, w1_gu, w1_down, g_mix, w_in, dw_weight, dw_bias,
           conv_ln_g, conv_ln_b, w_conv_out, w_alpha_f, b_alpha_f, w_alpha_b, b_alpha_b, gla_norm_g,
           w_gla_out, w_out, g_ffn2, w2_gu, w2_down, g_final):
    depth = w_mod.shape[0]
    assert depth == 1, "context-side mixing of non-final layers is not implemented"
    bsz, seq, d = x.shape
    d_conv = dw_weight.shape[2]
    qk_w = w_alpha_f.shape[2]
    v_w = gla_norm_g.shape[1]
    r = GLA_LOWRANK

    pad_rows = (-(bsz + 1)) % 8
    c_all = jnp.concatenate([c, c_ctx[None, :], jnp.zeros((pad_rows, d), F32)], axis=0)
    mod = _mod_call(c_all, w_mod[0], b_mod[0])
    mod_x = mod[:bsz].reshape(bsz, N_MOD, d)
    mod_c = mod[bsz:bsz + 1].reshape(1, N_MOD, d)

    w1gu, w1d = w1_gu[0].astype(BF16), w1_down[0].astype(BF16)
    w2gu, w2d = w2_gu[0].astype(BF16), w2_down[0].astype(BF16)
    wco, wgo, wo = w_conv_out[0].astype(BF16), w_gla_out[0].astype(BF16), w_out[0].astype(BF16)
    wi = w_in[0]
    o_q = 2 * d_conv
    o_k, o_v = o_q + qk_w, o_q + 2 * qk_w
    o_og = o_v + v_w
    o_af = o_og + v_w
    o_ga = o_af + 2 * r
    o_gb = o_ga + d
    w_x = jnp.concatenate([wi[:, :o_af], wi[:, o_ga:]], axis=1).astype(BF16)
    w_c = wi[:, o_k:o_og].astype(BF16)
    wlr = jnp.concatenate([wi[:, o_af:o_ga], jnp.zeros((d, LANE - 2 * r), F32)], axis=1).astype(BF16)
    wal = jnp.zeros((LANE, 2 * qk_w), F32)
    wal = wal.at[:r, :qk_w].set(w_alpha_f[0]).at[r:2 * r, qk_w:].set(w_alpha_b[0]).astype(BF16)
    bal = jnp.concatenate([b_alpha_f[0], b_alpha_b[0]])
    del o_gb

    x1 = _ffn_call(x, mod_x, True, g_ffn1[0], w1gu, w1d, 0, 512, name="ffn1_x")
    h1 = _ffn_call(ctx, mod_c, False, g_ffn1[0], w1gu, w1d, 0, ctx.shape[1], name="ffn1_ctx")

    segs_x = (("glu", d_conv), ("plain", qk_w), ("plain", qk_w), ("plain", v_w), ("plain", v_w),
              ("sigmoid", d), ("sigmoid", d))
    zc, q, k, v, og, sga, sgb, gd = _inproj_call(x1, mod_x, True, g_mix[0], w_x, wlr, wal, bal,
                                                 segs_x, 512, "inproj_x")
    segs_c = (("plain", qk_w), ("plain", v_w))
    kc, vc, gdc = _inproj_call(h1, mod_c, False, g_mix[0], w_c, wlr, wal, bal, segs_c,
                               ctx.shape[1], "inproj_ctx")

    o = _gla_call(q, k, v, gd, kc, vc, gdc, GLA_CHUNK)

    x2 = _merge_call(zc, o, og, sga, sgb, x1, mod_x, dw_weight[0], dw_bias[0], conv_ln_g[0],
                     conv_ln_b[0], gla_norm_g[0], wco, wgo, wo, 256)

    return _ffn_call(x2, mod_x, True, g_ffn2[0], w2gu, w2d, 6, 512, g_final=g_final, name="ffn2_x")
```

```python
import functools

import jax
import jax.numpy as jnp
from jax import lax
from jax.experimental import pallas as pl
from jax.experimental.pallas import tpu as pltpu

F32 = jnp.float32
BF16 = jnp.bfloat16

EPS = 1e-6
N_MOD = 9
GLA_HEADS = 4
GLA_TAU = 16.0
GLA_LOWRANK = 16
GLA_CHUNK = 128
LANE = 128
SUBLANE = 8
MXU_K = 256
VMEM_LIMIT = 56 * 1024 * 1024


def _dot(a, b):
    return jnp.dot(a, b, preferred_element_type=F32)


def _dot_t(a, b, ca, cb):
    return lax.dot_general(a, b, (((ca,), (cb,)), ((), ())), preferred_element_type=F32)


def _silu(x):
    return x * jax.nn.sigmoid(x)


def _rms(h, g):
    return h * lax.rsqrt(jnp.mean(h * h, axis=-1, keepdims=True) + EPS) * g


def _modulate(h, g, mod_ref, row):
    shift = mod_ref[0, row:row + 1, :]
    scale = mod_ref[0, row + 1:row + 2, :]
    return _rms(h, g) * (1.0 + scale) + shift


def _const_spec(shape):
    nd = len(shape)
    return pl.BlockSpec(shape, lambda *_: (0,) * nd, pipeline_mode=pl.Buffered(1))


def _params(sem):
    return pltpu.CompilerParams(dimension_semantics=sem, vmem_limit_bytes=VMEM_LIMIT)


def _mod_kernel(c_ref, w_ref, b_ref, o_ref):
    s = _silu(c_ref[...]).astype(BF16)
    o_ref[...] = _dot(s, w_ref[...].astype(BF16)) + b_ref[...]


def _mod_call(c_all, w_mod, b_mod, n_steps=8):
    rows, d = c_all.shape
    n = w_mod.shape[1]
    bn = n // n_steps
    return pl.pallas_call(
        _mod_kernel,
        grid=(n_steps,),
        in_specs=[pl.BlockSpec((rows, d), lambda j: (0, 0)),
                  pl.BlockSpec((d, bn), lambda j: (0, j)),
                  pl.BlockSpec((1, bn), lambda j: (0, j))],
        out_specs=pl.BlockSpec((rows, bn), lambda j: (0, j)),
        out_shape=jax.ShapeDtypeStruct((rows, n), F32),
        compiler_params=_params(("arbitrary",)),
        name="mod",
    )(c_all, w_mod, b_mod.reshape(1, n))


def _ffn_bounds(f, n_chunks):
    tiles = -(-f // MXU_K)
    edges = [min(f, MXU_K * (-(-tiles * j // n_chunks))) for j in range(n_chunks + 1)]
    return list(zip(edges[:-1], edges[1:]))


def _ffn_kernel(h_ref, mod_ref, g_ref, wgu_ref, wd_ref, *rest, mod_row, n_chunks, final_norm):
    if final_norm:
        gfin_ref, o_ref = rest
    else:
        (o_ref,) = rest
    h = h_ref[0]
    u = _modulate(h, g_ref[...], mod_ref, mod_row).astype(BF16)
    f = wd_ref.shape[0]
    acc = None
    for lo, hi in _ffn_bounds(f, n_chunks):
        a = _dot(u, wgu_ref[:, lo:hi])
        b = _dot(u, wgu_ref[:, f + lo:f + hi])
        m = (_silu(a) * b).astype(BF16)
        d = _dot(m, wd_ref[lo:hi, :])
        acc = d if acc is None else acc + d
    gate = mod_ref[0, mod_row + 2:mod_row + 3, :]
    y = h + (0.5 * gate) * acc
    if final_norm:
        y = _rms(y, gfin_ref[...])
    o_ref[0] = y


def _ffn_call(h, mod, mod_per_batch, g, wgu, wd, mod_row, tm, g_final=None, name="ffn"):
    bsz, s, d = h.shape
    f = wd.shape[0]
    mod_idx = (lambda b, j: (b, 0, 0)) if mod_per_batch else (lambda b, j: (0, 0, 0))
    in_specs = [pl.BlockSpec((1, tm, d), lambda b, j: (b, j, 0)),
                pl.BlockSpec((1, N_MOD, d), mod_idx),
                _const_spec((1, d)),
                _const_spec((d, 2 * f)),
                _const_spec((f, d))]
    args = [h, mod, g.reshape(1, d), wgu, wd]
    if g_final is not None:
        in_specs.append(_const_spec((1, d)))
        args.append(g_final.reshape(1, d))
    kern = functools.partial(_ffn_kernel, mod_row=mod_row, n_chunks=2, final_norm=g_final is not None)
    return pl.pallas_call(
        kern,
        grid=(bsz, s // tm),
        in_specs=in_specs,
        out_specs=pl.BlockSpec((1, tm, d), lambda b, j: (b, j, 0)),
        out_shape=jax.ShapeDtypeStruct((bsz, s, d), F32),
        compiler_params=_params(("parallel", "parallel")),
        name=name,
    )(*args)


def _inproj_kernel(x_ref, mod_ref, g_ref, w_ref, wlr_ref, wal_ref, bal_ref, *out_refs, segs):
    u = _modulate(x_ref[0], g_ref[...], mod_ref, 3).astype(BF16)
    off = 0
    for (kind, width), o_ref in zip(segs, out_refs[:-1]):
        if kind == "glu":
            a = _dot(u, w_ref[:, off:off + width])
            b = _dot(u, w_ref[:, off + width:off + 2 * width])
            o_ref[0] = (a * jax.nn.sigmoid(b)).astype(o_ref.dtype)
            off += 2 * width
        else:
            p = _dot(u, w_ref[:, off:off + width])
            if kind == "sigmoid":
                p = jax.nn.sigmoid(p)
            o_ref[0] = p.astype(o_ref.dtype)
            off += width
    lr = _dot(u, wlr_ref[...]).astype(BF16)
    z = _dot(lr, wal_ref[...]) + bal_ref[...]
    log_sig = jnp.minimum(z, 0.0) - jnp.log1p(jnp.exp(-jnp.abs(z)))
    out_refs[-1][0] = log_sig * (1.0 / GLA_TAU)


def _inproj_call(x, mod, mod_per_batch, g, w, wlr, wal, bal, segs, tm, name):
    bsz, s, d = x.shape
    mod_idx = (lambda b, j: (b, 0, 0)) if mod_per_batch else (lambda b, j: (0, 0, 0))
    n_dec = wal.shape[1]
    tok = lambda n: pl.BlockSpec((1, tm, n), lambda b, j: (b, j, 0))
    out_shape = [jax.ShapeDtypeStruct((bsz, s, width), BF16) for _, width in segs]
    out_shape.append(jax.ShapeDtypeStruct((bsz, s, n_dec), F32))
    out_specs = [tok(width) for _, width in segs] + [tok(n_dec)]
    return pl.pallas_call(
        functools.partial(_inproj_kernel, segs=segs),
        grid=(bsz, s // tm),
        in_specs=[tok(d),
                  pl.BlockSpec((1, N_MOD, d), mod_idx),
                  _const_spec((1, d)),
                  _const_spec(w.shape),
                  _const_spec(wlr.shape),
                  _const_spec(wal.shape),
                  _const_spec((1, n_dec))],
        out_specs=out_specs,
        out_shape=out_shape,
        compiler_params=_params(("parallel", "parallel")),
        name=name,
    )(x, mod, g.reshape(1, d), w, wlr, wal, bal.reshape(1, n_dec))


def _split3(g):
    hi = g.astype(BF16)
    r1 = g - hi.astype(F32)
    mid = r1.astype(BF16)
    lo = (r1 - mid.astype(F32)).astype(BF16)
    return [hi, mid, lo]


def _prefix_suffix(tri, gf, gb):
    c, w = gf.shape
    p = _dot(tri, jnp.concatenate(_split3(gf) + _split3(gb), axis=1))
    bf = p[:, 0:w] + p[:, w:2 * w] + p[:, 2 * w:3 * w]
    pb = p[:, 3 * w:4 * w] + p[:, 4 * w:5 * w] + p[:, 5 * w:6 * w]
    bb = (pb[c - 1:c, :] - pb) + gb
    return bf, bb


def _gla_kernel(q_ref, k_ref, v_ref, gf_ref, gb_ref, kc_ref, vc_ref, gfc_ref, gbc_ref, o_ref,
                att_s, qb_s, u_s, dec_s, st_s, *, chunk, scale):
    c = chunk
    dk = q_ref.shape[2]
    n = q_ref.shape[1] // c
    nc = kc_ref.shape[1] // c
    mid = c // 2
    row = lax.broadcasted_iota(jnp.int32, (c, c), 0)
    col = lax.broadcasted_iota(jnp.int32, (c, c), 1)
    lower = row >= col
    upper = row <= col
    tri = lower.astype(BF16)

    uc, decc = [], []
    for i in range(nc):
        sl = slice(i * c, (i + 1) * c)
        bf, bb = _prefix_suffix(tri, gfc_ref[0, sl, :], gbc_ref[0, sl, :])
        blf, blb = bf[c - 1:c, :], bb[0:1, :]
        kcf = kc_ref[0, sl, :].astype(F32)
        kl = jnp.concatenate([(kcf * jnp.exp(blf - bf)).astype(BF16),
                              (kcf * jnp.exp(blb - bb)).astype(BF16)], axis=1)
        uc.append(_dot_t(vc_ref[0, sl, :], kl, 0, 0))
        decc.append((jnp.exp(blf), jnp.exp(blb)))
    stf0 = jnp.zeros((uc[0].shape[0], dk), F32)
    stb0 = stf0
    for i in range(nc):
        j = nc - 1 - i
        stf0 = stf0 * decc[i][0] + uc[i][:, :dk]
        stb0 = stb0 * decc[j][1] + uc[j][:, dk:]

    def prep(i, carry):
        sl = pl.ds(pl.multiple_of(i * c, c), c)
        bf, bb = _prefix_suffix(tri, gf_ref[0, sl, :], gb_ref[0, sl, :])
        blf, blb = bf[c - 1:c, :], bb[0:1, :]
        bmf, bmb = bf[mid:mid + 1, :], bb[mid:mid + 1, :]
        q = q_ref[0, sl, :].astype(F32) * scale
        k = k_ref[0, sl, :].astype(F32)
        qmf = q * jnp.exp(bf - bmf)
        qmb = q * jnp.exp(bb - bmb)
        kmf = k * jnp.exp(bmf - bf)
        kmb = k * jnp.exp(bmb - bb)
        att = (jnp.where(lower, _dot_t(qmf.astype(BF16), kmf.astype(BF16), 1, 1), 0.0)
               + jnp.where(upper, _dot_t(qmb.astype(BF16), kmb.astype(BF16), 1, 1), 0.0))
        att_s[sl, :] = att.astype(BF16)
        qb_s[sl, :] = jnp.concatenate([(qmf * jnp.exp(bmf)).astype(BF16),
                                       (qmb * jnp.exp(bmb)).astype(BF16)], axis=1)
        kl = jnp.concatenate([(kmf * jnp.exp(blf - bmf)).astype(BF16),
                              (kmb * jnp.exp(blb - bmb)).astype(BF16)], axis=1)
        u_s[i] = _dot_t(v_ref[0, sl, :], kl, 0, 0)
        dec_s[i] = jnp.concatenate([jnp.exp(blf), jnp.exp(blb)], axis=1)
        return carry

    lax.fori_loop(0, n, prep, 0, unroll=4)

    def scan(i, carry):
        stf, stb = carry
        j = n - 1 - i
        st_s[i, :, 0:dk] = stf.astype(BF16)
        st_s[j, :, dk:2 * dk] = stb.astype(BF16)
        stf = stf * dec_s[i][:, 0:dk] + u_s[i, :, 0:dk]
        stb = stb * dec_s[j][:, dk:2 * dk] + u_s[j, :, dk:2 * dk]
        return stf, stb

    lax.fori_loop(0, n, scan, (stf0, stb0))

    def emit(i, carry):
        sl = pl.ds(pl.multiple_of(i * c, c), c)
        o_ref[0, sl, :] = _dot(att_s[sl, :], v_ref[0, sl, :]) + _dot_t(qb_s[sl, :], st_s[i], 1, 1)
        return carry

    lax.fori_loop(0, n, emit, 0, unroll=4)


def _gla_call(q, k, v, gd, kc, vc, gdc, chunk):
    bsz, s, qk_w = q.shape
    v_w = v.shape[2]
    sc = kc.shape[1]
    dk, dv = qk_w // GLA_HEADS, v_w // GLA_HEADS
    n = s // chunk
    assert s % chunk == 0 and sc % chunk == 0 and dk == LANE
    hd = lambda rows, w, shift=0: pl.BlockSpec((1, rows, w), lambda b, h: (b, 0, h + shift))
    return pl.pallas_call(
        functools.partial(_gla_kernel, chunk=chunk, scale=dk ** -0.5),
        grid=(bsz, GLA_HEADS),
        in_specs=[hd(s, dk), hd(s, dk), hd(s, dv), hd(s, dk), hd(s, dk, GLA_HEADS),
                  hd(sc, dk), hd(sc, dv), hd(sc, dk), hd(sc, dk, GLA_HEADS)],
        out_specs=hd(s, dv),
        out_shape=jax.ShapeDtypeStruct((bsz, s, v_w), F32),
        scratch_shapes=[pltpu.VMEM((s, chunk), BF16),
                        pltpu.VMEM((s, 2 * dk), BF16),
                        pltpu.VMEM((n, dv, 2 * dk), F32),
                        pltpu.VMEM((n, 1, 2 * dk), F32),
                        pltpu.VMEM((n, dv, 2 * dk), BF16)],
        compiler_params=_params(("parallel", "parallel")),
        name="gla",
    )(q, k, v, gd, gd, kc, vc, gdc, gdc)


def _depthwise_conv(win_ref, shift_ref, dww_ref, out_ref, halo, row_block):
    tm, dc = out_ref.shape
    width = dww_ref.shape[0]
    base = halo - width // 2
    n_a = -(-(width + base) // SUBLANE)
    rows = shift_ref.shape[1]
    for phase in range(1, SUBLANE):
        shift_ref[phase] = win_ref[phase:phase + rows, :]
    for c0 in range(0, dc, LANE):
        w = dww_ref[:, c0:c0 + LANE]
        for r0 in range(0, tm, row_block):
            acc = None
            for phase in range(SUBLANE):
                for a in range(n_a):
                    d = SUBLANE * a + phase - base
                    if not 0 <= d < width:
                        continue
                    lo = r0 + SUBLANE * a
                    if phase == 0:
                        z = win_ref[lo:lo + row_block, c0:c0 + LANE]
                    else:
                        z = shift_ref[phase, lo:lo + row_block, c0:c0 + LANE]
                    term = z * w[d:d + 1, :]
                    acc = term if acc is None else acc + term
            out_ref[r0:r0 + row_block, c0:c0 + LANE] = acc


def _merge_kernel(zc_ref, o_ref, og_ref, sga_ref, sgb_ref, x_ref, mod_ref, dww_ref, dwb_ref,
                  lng_ref, lnb_ref, gng_ref, wco_ref, wgo_ref, wo_ref, out_ref, win_ref, shift_ref, conv_ref,
                  *, halo):
    tm = x_ref.shape[1]
    seq = zc_ref.shape[1]
    j = pl.program_id(1)
    nj = pl.num_programs(1)
    t0 = pl.multiple_of(j * tm, tm)

    prev_lo = pl.multiple_of(jnp.maximum(t0 - halo, 0), halo)
    next_lo = pl.multiple_of(jnp.minimum(t0 + tm, seq - halo), halo)
    has_prev = (j > 0).astype(F32)
    has_next = (j < nj - 1).astype(F32)
    win_ref[0:halo, :] = zc_ref[0, pl.ds(prev_lo, halo), :].astype(F32) * has_prev
    win_ref[halo:halo + tm, :] = zc_ref[0, pl.ds(t0, tm), :].astype(F32)
    win_ref[halo + tm:, :] = zc_ref[0, pl.ds(next_lo, halo), :].astype(F32) * has_next

    _depthwise_conv(win_ref, shift_ref, dww_ref, conv_ref, halo, 8 * SUBLANE)
    z = conv_ref[...] + dwb_ref[...]
    mu = jnp.mean(z, axis=-1, keepdims=True)
    zc = z - mu
    var = jnp.mean(zc * zc, axis=-1, keepdims=True)
    z = zc * lax.rsqrt(var + EPS) * lng_ref[...] + lnb_ref[...]
    y_conv = _dot(_silu(z).astype(BF16), wco_ref[...])

    o = o_ref[0]
    dv = o.shape[1] // GLA_HEADS
    parts = []
    for h in range(GLA_HEADS):
        oh = o[:, h * dv:(h + 1) * dv]
        parts.append(oh * lax.rsqrt(jnp.mean(oh * oh, axis=-1, keepdims=True) + EPS))
    on = jnp.concatenate(parts, axis=1) * gng_ref[...]
    y_gla = _dot((on * _silu(og_ref[0].astype(F32))).astype(BF16), wgo_ref[...])

    merged = sga_ref[0].astype(F32) * y_conv + sgb_ref[0].astype(F32) * y_gla
    mix = _dot(merged.astype(BF16), wo_ref[...])
    out_ref[0] = x_ref[0] + mod_ref[0, 5:6, :] * mix


def _merge_call(zc, o, og, sga, sgb, x1, mod, dw_w, dw_b, ln_g, ln_b, gn_g, wco, wgo, wo, tm):
    bsz, s, d = x1.shape
    dc = zc.shape[2]
    halo = 2 * SUBLANE
    assert dw_w.shape[0] // 2 <= halo and s % tm == 0
    tok = lambda n: pl.BlockSpec((1, tm, n), lambda b, j: (b, j, 0))
    row = lambda a: a.reshape(1, -1)
    return pl.pallas_call(
        functools.partial(_merge_kernel, halo=halo),
        grid=(bsz, s // tm),
        in_specs=[pl.BlockSpec((1, s, dc), lambda b, j: (b, 0, 0)),
                  tok(o.shape[2]), tok(og.shape[2]), tok(d), tok(d), tok(d),
                  pl.BlockSpec((1, N_MOD, d), lambda b, j: (b, 0, 0)),
                  _const_spec(dw_w.shape), _const_spec((1, dc)), _const_spec((1, dc)), _const_spec((1, dc)),
                  _const_spec((1, o.shape[2])),
                  _const_spec(wco.shape), _const_spec(wgo.shape), _const_spec(wo.shape)],
        out_specs=tok(d),
        out_shape=jax.ShapeDtypeStruct((bsz, s, d), F32),
        scratch_shapes=[pltpu.VMEM((tm + 2 * halo, dc), F32),
                        pltpu.VMEM((SUBLANE, tm + 2 * halo - SUBLANE, dc), F32),
                        pltpu.VMEM((tm, dc), F32)],
        compiler_params=_params(("parallel", "arbitrary")),
        name="merge",
    )(zc, o, og, sga, sgb, x1, mod, dw_w, row(dw_b), row(ln_g), row(ln_b), row(gn_g), wco, wgo, wo)


def kernel(x, c, ctx, c_ctx, w_mod, b_mod, g_ffn1, w1_gu, w1_down, g_mix, w_in, dw_weight, dw_bias,
           conv_ln_g, conv_ln_b, w_conv_out, w_alpha_f, b_alpha_f, w_alpha_b, b_alpha_b, gla_norm_g,
           w_gla_out, w_out, g_ffn2, w2_gu, w2_down, g_final):
    depth = w_mod.shape[0]
    assert depth == 1, "context-side mixing of non-final layers is not implemented"
    bsz, seq, d = x.shape
    d_conv = dw_weight.shape[2]
    qk_w = w_alpha_f.shape[2]
    v_w = gla_norm_g.shape[1]
    r = GLA_LOWRANK

    pad_rows = (-(bsz + 1)) % SUBLANE
    c_all = jnp.concatenate([c, c_ctx[None, :], jnp.zeros((pad_rows, d), F32)], axis=0)
    mod = _mod_call(c_all, w_mod[0], b_mod[0])
    mod_x = mod[:bsz].reshape(bsz, N_MOD, d)
    mod_c = mod[bsz:bsz + 1].reshape(1, N_MOD, d)

    w1gu, w1d = w1_gu[0].astype(BF16), w1_down[0].astype(BF16)
    w2gu, w2d = w2_gu[0].astype(BF16), w2_down[0].astype(BF16)
    wco, wgo, wo = w_conv_out[0].astype(BF16), w_gla_out[0].astype(BF16), w_out[0].astype(BF16)
    wi = w_in[0]
    o_q = 2 * d_conv
    o_k, o_v = o_q + qk_w, o_q + 2 * qk_w
    o_og = o_v + v_w
    o_af = o_og + v_w
    o_ga = o_af + 2 * r
    w_x = jnp.concatenate([wi[:, :o_af], wi[:, o_ga:]], axis=1).astype(BF16)
    w_c = wi[:, o_k:o_og].astype(BF16)
    wlr = jnp.concatenate([wi[:, o_af:o_ga], jnp.zeros((d, LANE - 2 * r), F32)], axis=1).astype(BF16)
    wal = jnp.zeros((LANE, 2 * qk_w), F32)
    wal = wal.at[:r, :qk_w].set(w_alpha_f[0]).at[r:2 * r, qk_w:].set(w_alpha_b[0]).astype(BF16)
    bal = jnp.concatenate([b_alpha_f[0], b_alpha_b[0]])

    x1 = _ffn_call(x, mod_x, True, g_ffn1[0], w1gu, w1d, 0, 512, name="ffn1_x")
    h1 = _ffn_call(ctx, mod_c, False, g_ffn1[0], w1gu, w1d, 0, ctx.shape[1], name="ffn1_ctx")

    segs_x = (("glu", d_conv), ("plain", qk_w), ("plain", qk_w), ("plain", v_w), ("plain", v_w),
              ("sigmoid", d), ("sigmoid", d))
    zc, q, k, v, og, sga, sgb, gd = _inproj_call(x1, mod_x, True, g_mix[0], w_x, wlr, wal, bal,
                                                 segs_x, 512, "inproj_x")
    segs_c = (("plain", qk_w), ("plain", v_w))
    kc, vc, gdc = _inproj_call(h1, mod_c, False, g_mix[0], w_c, wlr, wal, bal, segs_c,
                               ctx.shape[1], "inproj_ctx")

    o = _gla_call(q, k, v, gd, kc, vc, gdc, GLA_CHUNK)

    x2 = _merge_call(zc, o, og, sga, sgb, x1, mod_x, dw_weight[0], dw_bias[0], conv_ln_g[0],
                     conv_ln_b[0], gla_norm_g[0], wco, wgo, wo, 256)

    return _ffn_call(x2, mod_x, True, g_ffn2[0], w2gu, w2d, 6, 512, g_final=g_final, name="ffn2_x")
```

```python
import functools

import jax
import jax.numpy as jnp
from jax import lax
from jax.experimental import pallas as pl
from jax.experimental.pallas import tpu as pltpu

F32 = jnp.float32
BF16 = jnp.bfloat16

EPS = 1e-6
N_MOD = 9
GLA_HEADS = 4
GLA_TAU = 16.0
GLA_LOWRANK = 16
GLA_CHUNK = 128
LANE = 128
SUBLANE = 8
MXU_K = 256
FFN_CHUNKS = 2
VMEM_LIMIT = 56 * 1024 * 1024


def _dot(a, b):
    return jnp.dot(a, b, preferred_element_type=F32)


def _dot_t(a, b, ca, cb):
    return lax.dot_general(a, b, (((ca,), (cb,)), ((), ())), preferred_element_type=F32)


def _silu(x):
    return x * jax.nn.sigmoid(x)


def _rms(h, g):
    return h * lax.rsqrt(jnp.mean(h * h, axis=-1, keepdims=True) + EPS) * g


def _modulate(h, g, mod_ref, row):
    shift = mod_ref[0, row:row + 1, :]
    scale = mod_ref[0, row + 1:row + 2, :]
    return _rms(h, g) * (1.0 + scale) + shift


def _const_spec(shape):
    nd = len(shape)
    return pl.BlockSpec(shape, lambda *_: (0,) * nd, pipeline_mode=pl.Buffered(1))


def _params(sem, flags=None):
    return pltpu.CompilerParams(dimension_semantics=sem, vmem_limit_bytes=VMEM_LIMIT, flags=flags)


def _mod_kernel(c_ref, w_ref, b_ref, o_ref):
    s = _silu(c_ref[...]).astype(BF16)
    o_ref[...] = _dot(s, w_ref[...].astype(BF16)) + b_ref[...]


def _mod_call(c_all, w_mod, b_mod, n_steps=8):
    rows, d = c_all.shape
    n = w_mod.shape[1]
    bn = n // n_steps
    return pl.pallas_call(
        _mod_kernel,
        grid=(n_steps,),
        in_specs=[pl.BlockSpec((rows, d), lambda j: (0, 0)),
                  pl.BlockSpec((d, bn), lambda j: (0, j)),
                  pl.BlockSpec((1, bn), lambda j: (0, j))],
        out_specs=pl.BlockSpec((rows, bn), lambda j: (0, j)),
        out_shape=jax.ShapeDtypeStruct((rows, n), F32),
        compiler_params=_params(("arbitrary",)),
        name="mod",
    )(c_all, w_mod, b_mod.reshape(1, n))


def _ffn_bounds(f, n_chunks):
    tiles = -(-f // MXU_K)
    edges = [min(f, MXU_K * (-(-tiles * j // n_chunks))) for j in range(n_chunks + 1)]
    return list(zip(edges[:-1], edges[1:]))


def _half_ffn(h, mod_ref, g_ref, wgu_ref, wd_ref, mod_row, n_chunks):
    u = _modulate(h, g_ref[...], mod_ref, mod_row).astype(BF16)
    f = wd_ref.shape[0]
    acc = None
    for lo, hi in _ffn_bounds(f, n_chunks):
        a = _dot(u, wgu_ref[:, lo:hi])
        b = _dot(u, wgu_ref[:, f + lo:f + hi])
        m = (_silu(a) * b).astype(BF16)
        d = _dot(m, wd_ref[lo:hi, :])
        acc = d if acc is None else acc + d
    gate = mod_ref[0, mod_row + 2:mod_row + 3, :]
    return h + (0.5 * gate) * acc


def _ffn_kernel(h_ref, mod_ref, g_ref, wgu_ref, wd_ref, *rest, mod_row, n_chunks):
    gfin_ref, o_ref = rest if len(rest) == 2 else (None, rest[0])
    y = _half_ffn(h_ref[0], mod_ref, g_ref, wgu_ref, wd_ref, mod_row, n_chunks)
    o_ref[0] = y if gfin_ref is None else _rms(y, gfin_ref[...])


def _ffn_call(h, mod, mod_per_batch, g, wgu, wd, mod_row, tm, name, g_final=None):
    bsz, s, d = h.shape
    f = wd.shape[0]
    mod_idx = (lambda b, j: (b, 0, 0)) if mod_per_batch else (lambda b, j: (0, 0, 0))
    in_specs = [pl.BlockSpec((1, tm, d), lambda b, j: (b, j, 0)),
                pl.BlockSpec((1, N_MOD, d), mod_idx),
                _const_spec((1, d)),
                _const_spec((d, 2 * f)),
                _const_spec((f, d))]
    args = [h, mod, g.reshape(1, d), wgu, wd]
    if g_final is not None:
        in_specs.append(_const_spec((1, d)))
        args.append(g_final.reshape(1, d))
    return pl.pallas_call(
        functools.partial(_ffn_kernel, mod_row=mod_row, n_chunks=FFN_CHUNKS),
        grid=(bsz, s // tm),
        in_specs=in_specs,
        out_specs=pl.BlockSpec((1, tm, d), lambda b, j: (b, j, 0)),
        out_shape=jax.ShapeDtypeStruct((bsz, s, d), F32),
        compiler_params=_params(("parallel", "parallel")),
        name=name,
    )(*args)


def _chunk_scans(g, chunk, o_ref):
    tm, n = g.shape
    steps = [1 << s for s in range(chunk.bit_length() - 1)]
    pos = lax.broadcasted_iota(jnp.int32, (tm, LANE), 0) % chunk
    take_prev = [pos >= d for d in steps]
    take_next = [pos < chunk - d for d in steps]
    for c0 in range(0, n, LANE):
        x = g[:, c0:c0 + LANE]
        for i, d in enumerate(steps):
            if c0 < n // 2:
                x = x + jnp.where(take_prev[i], pltpu.roll(x, d, axis=0), 0.0)
            else:
                x = x + jnp.where(take_next[i], pltpu.roll(x, tm - d, axis=0), 0.0)
        o_ref[0, :, c0:c0 + LANE] = x


def _inproj_kernel(x_ref, mod_ref, g_ref, w_ref, wlr_ref, wal_ref, bal_ref, *out_refs, segs, chunk):
    u = _modulate(x_ref[0], g_ref[...], mod_ref, 3).astype(BF16)

    def decay():
        lr = _dot(u, wlr_ref[...]).astype(BF16)
        z = _dot(lr, wal_ref[...]) + bal_ref[...]
        log_sig = jnp.minimum(z, 0.0) - jnp.log1p(jnp.exp(-jnp.abs(z)))
        _chunk_scans(log_sig * (1.0 / GLA_TAU), chunk, out_refs[-1])

    off = 0
    for idx, ((kind, width), o_ref) in enumerate(zip(segs, out_refs[:-1])):
        if idx == len(segs) // 2:
            decay()
        if kind == "glu":
            a = _dot(u, w_ref[:, off:off + width])
            b = _dot(u, w_ref[:, off + width:off + 2 * width])
            o_ref[0] = (a * jax.nn.sigmoid(b)).astype(o_ref.dtype)
            off += 2 * width
        else:
            p = _dot(u, w_ref[:, off:off + width])
            if kind == "sigmoid":
                p = jax.nn.sigmoid(p)
            o_ref[0] = p.astype(o_ref.dtype)
            off += width


def _inproj_call(x, mod, mod_per_batch, g, w, wlr, wal, bal, segs, tm, chunk, name):
    bsz, s, d = x.shape
    assert tm % chunk == 0 and chunk & (chunk - 1) == 0
    mod_idx = (lambda b, j: (b, 0, 0)) if mod_per_batch else (lambda b, j: (0, 0, 0))
    n_dec = wal.shape[1]
    tok = lambda n: pl.BlockSpec((1, tm, n), lambda b, j: (b, j, 0))
    out_shape = [jax.ShapeDtypeStruct((bsz, s, width), BF16) for _, width in segs]
    out_shape.append(jax.ShapeDtypeStruct((bsz, s, n_dec), F32))
    out_specs = [tok(width) for _, width in segs] + [tok(n_dec)]
    return pl.pallas_call(
        functools.partial(_inproj_kernel, segs=segs, chunk=chunk),
        grid=(bsz, s // tm),
        in_specs=[tok(d),
                  pl.BlockSpec((1, N_MOD, d), mod_idx),
                  _const_spec((1, d)),
                  _const_spec(w.shape),
                  _const_spec(wlr.shape),
                  _const_spec(wal.shape),
                  _const_spec((1, n_dec))],
        out_specs=out_specs,
        out_shape=out_shape,
        compiler_params=_params(("parallel", "parallel")),
        name=name,
    )(x, mod, g.reshape(1, d), w, wlr, wal, bal.reshape(1, n_dec))


def _chunk_rows(i, c):
    return pl.ds(i * c, c) if isinstance(i, int) else pl.ds(pl.multiple_of(i * c, c), c)


def _gla_kernel(q_ref, k_ref, v_ref, bf_ref, bb_ref, kc_ref, vc_ref, bfc_ref, bbc_ref, o_ref,
                qm_s, km_s, kl_s, att_s, qb_s, u_s, dec_s, st_s, *, chunk, scale):
    c = chunk
    dk = q_ref.shape[2]
    n = q_ref.shape[1] // c
    nc = kc_ref.shape[1] // c
    mid = c // 2
    row = lax.broadcasted_iota(jnp.int32, (c, c), 0)
    col = lax.broadcasted_iota(jnp.int32, (c, c), 1)
    lower = row >= col
    upper = row <= col

    uc, decc = [], []
    for i in range(nc):
        sl = slice(i * c, (i + 1) * c)
        bf, bb = bfc_ref[0, sl, :], bbc_ref[0, sl, :]
        blf, blb = bf[c - 1:c, :], bb[0:1, :]
        kcf = kc_ref[0, sl, :].astype(F32)
        kl = jnp.concatenate([(kcf * jnp.exp(blf - bf)).astype(BF16),
                              (kcf * jnp.exp(blb - bb)).astype(BF16)], axis=1)
        uc.append(_dot_t(vc_ref[0, sl, :], kl, 0, 0))
        decc.append((jnp.exp(blf), jnp.exp(blb)))
    stf0 = jnp.zeros((uc[0].shape[0], dk), F32)
    stb0 = stf0
    for i in range(nc):
        j = nc - 1 - i
        stf0 = stf0 * decc[i][0] + uc[i][:, :dk]
        stb0 = stb0 * decc[j][1] + uc[j][:, dk:]

    def factors(i):
        sl = _chunk_rows(i, c)
        bf, bb = bf_ref[0, sl, :], bb_ref[0, sl, :]
        blf, blb = bf[c - 1:c, :], bb[0:1, :]
        bmf, bmb = bf[mid:mid + 1, :], bb[mid:mid + 1, :]
        q = q_ref[0, sl, :].astype(F32) * scale
        k = k_ref[0, sl, :].astype(F32)
        qmf = q * jnp.exp(bf - bmf)
        qmb = q * jnp.exp(bb - bmb)
        kmf = k * jnp.exp(bmf - bf)
        kmb = k * jnp.exp(bmb - bb)
        qm_s[sl, :] = jnp.concatenate([qmf.astype(BF16), qmb.astype(BF16)], axis=1)
        km_s[sl, :] = jnp.concatenate([kmf.astype(BF16), kmb.astype(BF16)], axis=1)
        qb_s[sl, :] = jnp.concatenate([(qmf * jnp.exp(bmf)).astype(BF16),
                                       (qmb * jnp.exp(bmb)).astype(BF16)], axis=1)
        kl_s[sl, :] = jnp.concatenate([(kmf * jnp.exp(blf - bmf)).astype(BF16),
                                       (kmb * jnp.exp(blb - bmb)).astype(BF16)], axis=1)
        dec_s[i] = jnp.concatenate([jnp.exp(blf), jnp.exp(blb)], axis=1)

    def products(i):
        sl = _chunk_rows(i, c)
        qm, km = qm_s[sl, :], km_s[sl, :]
        att = (jnp.where(lower, _dot_t(qm[:, 0:dk], km[:, 0:dk], 1, 1), 0.0)
               + jnp.where(upper, _dot_t(qm[:, dk:2 * dk], km[:, dk:2 * dk], 1, 1), 0.0))
        att_s[sl, :] = att.astype(BF16)
        u_s[i] = _dot_t(v_ref[0, sl, :], kl_s[sl, :], 0, 0)

    def prep(i, carry):
        products(i - 1)
        factors(i)
        return carry

    factors(0)
    lax.fori_loop(1, n, prep, 0, unroll=3)
    products(n - 1)

    def scan(i, carry):
        stf, stb = carry
        j = n - 1 - i
        st_s[i, :, 0:dk] = stf.astype(BF16)
        st_s[j, :, dk:2 * dk] = stb.astype(BF16)
        stf = stf * dec_s[i][:, 0:dk] + u_s[i, :, 0:dk]
        stb = stb * dec_s[j][:, dk:2 * dk] + u_s[j, :, dk:2 * dk]
        return stf, stb

    lax.fori_loop(0, n, scan, (stf0, stb0))

    def emit(i, carry):
        sl = _chunk_rows(i, c)
        o_ref[0, sl, :] = _dot(att_s[sl, :], v_ref[0, sl, :]) + _dot_t(qb_s[sl, :], st_s[i], 1, 1)
        return carry

    lax.fori_loop(0, n, emit, 0, unroll=4)


def _gla_call(q, k, v, gd, kc, vc, gdc, chunk):
    bsz, s, qk_w = q.shape
    v_w = v.shape[2]
    sc = kc.shape[1]
    dk, dv = qk_w // GLA_HEADS, v_w // GLA_HEADS
    n = s // chunk
    assert s % chunk == 0 and sc % chunk == 0 and dk == LANE
    hd = lambda rows, w, shift=0: pl.BlockSpec((1, rows, w), lambda b, h: (b, 0, h + shift))
    return pl.pallas_call(
        functools.partial(_gla_kernel, chunk=chunk, scale=dk ** -0.5),
        grid=(bsz, GLA_HEADS),
        in_specs=[hd(s, dk), hd(s, dk), hd(s, dv), hd(s, dk), hd(s, dk, GLA_HEADS),
                  hd(sc, dk), hd(sc, dv), hd(sc, dk), hd(sc, dk, GLA_HEADS)],
        out_specs=hd(s, dv),
        out_shape=jax.ShapeDtypeStruct((bsz, s, v_w), F32),
        scratch_shapes=[pltpu.VMEM((s, 2 * dk), BF16),
                        pltpu.VMEM((s, 2 * dk), BF16),
                        pltpu.VMEM((s, 2 * dk), BF16),
                        pltpu.VMEM((s, chunk), BF16),
                        pltpu.VMEM((s, 2 * dk), BF16),
                        pltpu.VMEM((n, dv, 2 * dk), F32),
                        pltpu.VMEM((n, 1, 2 * dk), F32),
                        pltpu.VMEM((n, dv, 2 * dk), BF16)],
        compiler_params=_params(("parallel", "parallel")),
        name="gla",
    )(q, k, v, gd, gd, kc, vc, gdc, gdc)


def _depthwise_conv(win_ref, shift_ref, dww_ref, out_ref, halo, row_block):
    tm, dc = out_ref.shape
    width = dww_ref.shape[0]
    base = halo - width // 2
    n_a = -(-(width + base) // SUBLANE)
    rows = shift_ref.shape[2]
    for ci, c0 in enumerate(range(0, dc, LANE)):
        sh = shift_ref.at[ci % shift_ref.shape[0]]
        for phase in range(1, SUBLANE):
            sh[phase - 1] = win_ref[phase:phase + rows, c0:c0 + LANE]
        w = dww_ref[:, c0:c0 + LANE]
        for r0 in range(0, tm, row_block):
            acc = None
            for phase in range(SUBLANE):
                for a in range(n_a):
                    d = SUBLANE * a + phase - base
                    if not 0 <= d < width:
                        continue
                    lo = r0 + SUBLANE * a
                    if phase == 0:
                        z = win_ref[lo:lo + row_block, c0:c0 + LANE]
                    else:
                        z = sh[phase - 1, lo:lo + row_block, :]
                    term = z * w[d:d + 1, :]
                    acc = term if acc is None else acc + term
            out_ref[r0:r0 + row_block, c0:c0 + LANE] = acc


def _conv_branch(zc_ref, j, nj, tm, halo, win_ref, shift_ref, conv_ref, dww_ref, dwb_ref, lng_ref, lnb_ref):
    seq = zc_ref.shape[1]
    t0 = pl.multiple_of(j * tm, tm)
    prev_lo = pl.multiple_of(jnp.maximum(t0 - halo, 0), halo)
    next_lo = pl.multiple_of(jnp.minimum(t0 + tm, seq - halo), halo)
    has_prev = (j > 0).astype(F32)
    has_next = (j < nj - 1).astype(F32)
    win_ref[0:halo, :] = zc_ref[0, pl.ds(prev_lo, halo), :].astype(F32) * has_prev
    win_ref[halo:halo + tm, :] = zc_ref[0, pl.ds(t0, tm), :].astype(F32)
    win_ref[halo + tm:, :] = zc_ref[0, pl.ds(next_lo, halo), :].astype(F32) * has_next

    _depthwise_conv(win_ref, shift_ref, dww_ref, conv_ref, halo, 4 * SUBLANE)
    z = conv_ref[...] + dwb_ref[...]
    mu = jnp.mean(z, axis=-1, keepdims=True)
    zc = z - mu
    var = jnp.mean(zc * zc, axis=-1, keepdims=True)
    z = zc * lax.rsqrt(var + EPS) * lng_ref[...] + lnb_ref[...]
    return _silu(z).astype(BF16)


def _merge_kernel(zc_ref, o_ref, og_ref, sga_ref, sgb_ref, x_ref, mod_ref, dww_ref, dwb_ref, lng_ref, lnb_ref,
                  gng_ref, wco_ref, wgo_ref, wo_ref, out_ref, win_ref, shift_ref, conv_ref, *, halo):
    tm = x_ref.shape[1]
    act = _conv_branch(zc_ref, pl.program_id(1), pl.num_programs(1), tm, halo, win_ref, shift_ref, conv_ref,
                       dww_ref, dwb_ref, lng_ref, lnb_ref)
    y_conv = _dot(act, wco_ref[...])

    o = o_ref[0]
    dv = o.shape[1] // GLA_HEADS
    parts = []
    for h in range(GLA_HEADS):
        oh = o[:, h * dv:(h + 1) * dv]
        parts.append(oh * lax.rsqrt(jnp.mean(oh * oh, axis=-1, keepdims=True) + EPS))
    on = jnp.concatenate(parts, axis=1) * gng_ref[...]
    y_gla = _dot((on * _silu(og_ref[0].astype(F32))).astype(BF16), wgo_ref[...])

    merged = sga_ref[0].astype(F32) * y_conv + sgb_ref[0].astype(F32) * y_gla
    mix = _dot(merged.astype(BF16), wo_ref[...])
    out_ref[0] = x_ref[0] + mod_ref[0, 5:6, :] * mix


def _merge_call(zc, o, og, sga, sgb, x1, mod, dw_w, dw_b, ln_g, ln_b, gn_g, wco, wgo, wo, tm):
    bsz, s, d = x1.shape
    dc = zc.shape[2]
    halo = 2 * SUBLANE
    assert dw_w.shape[0] // 2 <= halo and s % tm == 0
    tok = lambda n: pl.BlockSpec((1, tm, n), lambda b, j: (b, j, 0))
    row = lambda a: a.reshape(1, -1)
    return pl.pallas_call(
        functools.partial(_merge_kernel, halo=halo),
        grid=(bsz, s // tm),
        in_specs=[pl.BlockSpec((1, s, dc), lambda b, j: (b, 0, 0)),
                  tok(o.shape[2]), tok(og.shape[2]), tok(d), tok(d), tok(d),
                  pl.BlockSpec((1, N_MOD, d), lambda b, j: (b, 0, 0)),
                  _const_spec(dw_w.shape), _const_spec((1, dc)), _const_spec((1, dc)), _const_spec((1, dc)),
                  _const_spec((1, o.shape[2])),
                  _const_spec(wco.shape), _const_spec(wgo.shape), _const_spec(wo.shape)],
        out_specs=tok(d),
        out_shape=jax.ShapeDtypeStruct((bsz, s, d), F32),
        scratch_shapes=[pltpu.VMEM((tm + 2 * halo, dc), F32),
                        pltpu.VMEM((2, SUBLANE - 1, tm + 2 * halo - SUBLANE, LANE), F32),
                        pltpu.VMEM((tm, dc), F32)],
        compiler_params=_params(("parallel", "arbitrary")),
        name="merge",
    )(zc, o, og, sga, sgb, x1, mod, dw_w, row(dw_b), row(ln_g), row(ln_b), row(gn_g), wco, wgo, wo)


def kernel(x, c, ctx, c_ctx, w_mod, b_mod, g_ffn1, w1_gu, w1_down, g_mix, w_in, dw_weight, dw_bias,
           conv_ln_g, conv_ln_b, w_conv_out, w_alpha_f, b_alpha_f, w_alpha_b, b_alpha_b, gla_norm_g,
           w_gla_out, w_out, g_ffn2, w2_gu, w2_down, g_final):
    depth = w_mod.shape[0]
    assert depth == 1, "context-side mixing of non-final layers is not implemented"
    bsz, seq, d = x.shape
    d_conv = dw_weight.shape[2]
    qk_w = w_alpha_f.shape[2]
    v_w = gla_norm_g.shape[1]
    r = GLA_LOWRANK

    pad_rows = (-(bsz + 1)) % SUBLANE
    c_all = jnp.concatenate([c, c_ctx[None, :], jnp.zeros((pad_rows, d), F32)], axis=0)
    mod = _mod_call(c_all, w_mod[0], b_mod[0])
    mod_x = mod[:bsz].reshape(bsz, N_MOD, d)
    mod_c = mod[bsz:bsz + 1].reshape(1, N_MOD, d)

    w1gu, w1d = w1_gu[0].astype(BF16), w1_down[0].astype(BF16)
    w2gu, w2d = w2_gu[0].astype(BF16), w2_down[0].astype(BF16)
    wco, wgo, wo = w_conv_out[0].astype(BF16), w_gla_out[0].astype(BF16), w_out[0].astype(BF16)
    wi = w_in[0]
    o_q = 2 * d_conv
    o_k, o_v = o_q + qk_w, o_q + 2 * qk_w
    o_og = o_v + v_w
    o_af = o_og + v_w
    o_ga = o_af + 2 * r
    w_x = jnp.concatenate([wi[:, :o_af], wi[:, o_ga:]], axis=1).astype(BF16)
    w_c = wi[:, o_k:o_og].astype(BF16)
    wlr = jnp.concatenate([wi[:, o_af:o_ga], jnp.zeros((d, LANE - 2 * r), F32)], axis=1).astype(BF16)
    wal = jnp.zeros((LANE, 2 * qk_w), F32)
    wal = wal.at[:r, :qk_w].set(w_alpha_f[0]).at[r:2 * r, qk_w:].set(w_alpha_b[0]).astype(BF16)
    bal = jnp.concatenate([b_alpha_f[0], b_alpha_b[0]])

    x1 = _ffn_call(x, mod_x, True, g_ffn1[0], w1gu, w1d, 0, 512, name="ffn1_x")
    h1 = _ffn_call(ctx, mod_c, False, g_ffn1[0], w1gu, w1d, 0, ctx.shape[1], name="ffn1_ctx")

    segs_x = (("glu", d_conv), ("plain", qk_w), ("plain", qk_w), ("plain", v_w), ("plain", v_w),
              ("sigmoid", d), ("sigmoid", d))
    zc, q, k, v, og, sga, sgb, gd = _inproj_call(x1, mod_x, True, g_mix[0], w_x, wlr, wal, bal,
                                                 segs_x, 512, GLA_CHUNK, "inproj_x")
    segs_c = (("plain", qk_w), ("plain", v_w))
    kc, vc, gdc = _inproj_call(h1, mod_c, False, g_mix[0], w_c, wlr, wal, bal, segs_c,
                               ctx.shape[1], GLA_CHUNK, "inproj_ctx")

    o = _gla_call(q, k, v, gd, kc, vc, gdc, GLA_CHUNK)

    x2 = _merge_call(zc, o, og, sga, sgb, x1, mod_x, dw_weight[0], dw_bias[0], conv_ln_g[0],
                     conv_ln_b[0], gla_norm_g[0], wco, wgo, wo, 256)

    return _ffn_call(x2, mod_x, True, g_ffn2[0], w2gu, w2d, 6, 512, "ffn2_x", g_final=g_final)
```

```python
import functools

import jax
import jax.numpy as jnp
from jax import lax
from jax.experimental import pallas as pl
from jax.experimental.pallas import tpu as pltpu

F32 = jnp.float32
BF16 = jnp.bfloat16

EPS = 1e-6
N_MOD = 9
GLA_HEADS = 4
GLA_TAU = 16.0
GLA_LOWRANK = 16
GLA_CHUNK = 128
LANE = 128
SUBLANE = 8
MXU_K = 256
FFN_CHUNKS = 2
VMEM_LIMIT = 56 * 1024 * 1024


def _dot(a, b):
    return jnp.dot(a, b, preferred_element_type=F32)


def _dot_t(a, b, ca, cb):
    return lax.dot_general(a, b, (((ca,), (cb,)), ((), ())), preferred_element_type=F32)


def _silu(x):
    return x * jax.nn.sigmoid(x)


def _rms(h, g):
    return h * lax.rsqrt(jnp.mean(h * h, axis=-1, keepdims=True) + EPS) * g


def _modulate(h, g, mod_ref, row):
    shift = mod_ref[0, row:row + 1, :]
    scale = mod_ref[0, row + 1:row + 2, :]
    return _rms(h, g) * (1.0 + scale) + shift


def _const_spec(shape):
    nd = len(shape)
    return pl.BlockSpec(shape, lambda *_: (0,) * nd, pipeline_mode=pl.Buffered(1))


def _params(sem, flags=None):
    return pltpu.CompilerParams(dimension_semantics=sem, vmem_limit_bytes=VMEM_LIMIT, flags=flags)


def _mod_kernel(c_ref, w_ref, b_ref, o_ref):
    s = _silu(c_ref[...]).astype(BF16)
    o_ref[...] = _dot(s, w_ref[...].astype(BF16)) + b_ref[...]


def _mod_call(c_all, w_mod, b_mod, n_steps=8):
    rows, d = c_all.shape
    n = w_mod.shape[1]
    bn = n // n_steps
    return pl.pallas_call(
        _mod_kernel,
        grid=(n_steps,),
        in_specs=[pl.BlockSpec((rows, d), lambda j: (0, 0)),
                  pl.BlockSpec((d, bn), lambda j: (0, j)),
                  pl.BlockSpec((1, bn), lambda j: (0, j))],
        out_specs=pl.BlockSpec((rows, bn), lambda j: (0, j)),
        out_shape=jax.ShapeDtypeStruct((rows, n), F32),
        compiler_params=_params(("arbitrary",)),
        name="mod",
    )(c_all, w_mod, b_mod.reshape(1, n))


def _ffn_bounds(f, n_chunks):
    tiles = -(-f // MXU_K)
    edges = [min(f, MXU_K * (-(-tiles * j // n_chunks))) for j in range(n_chunks + 1)]
    return list(zip(edges[:-1], edges[1:]))


def _half_ffn(h, mod_ref, g_ref, wgu_ref, wd_ref, mod_row, n_chunks):
    u = _modulate(h, g_ref[...], mod_ref, mod_row).astype(BF16)
    f = wd_ref.shape[0]
    acc = None
    for lo, hi in _ffn_bounds(f, n_chunks):
        a = _dot(u, wgu_ref[:, lo:hi])
        b = _dot(u, wgu_ref[:, f + lo:f + hi])
        m = (_silu(a) * b).astype(BF16)
        d = _dot(m, wd_ref[lo:hi, :])
        acc = d if acc is None else acc + d
    gate = mod_ref[0, mod_row + 2:mod_row + 3, :]
    return h + (0.5 * gate) * acc


def _ffn_kernel(h_ref, mod_ref, g_ref, wgu_ref, wd_ref, *rest, mod_row, n_chunks):
    gfin_ref, o_ref = rest if len(rest) == 2 else (None, rest[0])
    y = _half_ffn(h_ref[0], mod_ref, g_ref, wgu_ref, wd_ref, mod_row, n_chunks)
    o_ref[0] = y if gfin_ref is None else _rms(y, gfin_ref[...])


def _ffn_call(h, mod, mod_per_batch, g, wgu, wd, mod_row, tm, name, g_final=None):
    bsz, s, d = h.shape
    f = wd.shape[0]
    mod_idx = (lambda b, j: (b, 0, 0)) if mod_per_batch else (lambda b, j: (0, 0, 0))
    in_specs = [pl.BlockSpec((1, tm, d), lambda b, j: (b, j, 0)),
                pl.BlockSpec((1, N_MOD, d), mod_idx),
                _const_spec((1, d)),
                _const_spec((d, 2 * f)),
                _const_spec((f, d))]
    args = [h, mod, g.reshape(1, d), wgu, wd]
    if g_final is not None:
        in_specs.append(_const_spec((1, d)))
        args.append(g_final.reshape(1, d))
    return pl.pallas_call(
        functools.partial(_ffn_kernel, mod_row=mod_row, n_chunks=FFN_CHUNKS),
        grid=(bsz, s // tm),
        in_specs=in_specs,
        out_specs=pl.BlockSpec((1, tm, d), lambda b, j: (b, j, 0)),
        out_shape=jax.ShapeDtypeStruct((bsz, s, d), F32),
        compiler_params=_params(("parallel", "parallel")),
        name=name,
    )(*args)


def _scan_rows(g, forward):
    n_tiles = g.shape[0] // SUBLANE
    rid = lax.broadcasted_iota(jnp.int32, (SUBLANE, LANE), 0)
    tiles = [g[SUBLANE * t:SUBLANE * (t + 1), :] for t in range(n_tiles)]
    for d in [1 << s for s in range(SUBLANE.bit_length() - 1)]:
        if forward:
            tiles = [x + jnp.where(rid >= d, pltpu.roll(x, d, axis=0), 0.0) for x in tiles]
        else:
            tiles = [x + jnp.where(rid < SUBLANE - d, pltpu.roll(x, SUBLANE - d, axis=0), 0.0) for x in tiles]
    order = range(n_tiles) if forward else range(n_tiles - 1, -1, -1)
    edge = SUBLANE - 1 if forward else 0
    carry = None
    for t in order:
        if carry is not None:
            tiles[t] = tiles[t] + carry
        carry = jnp.broadcast_to(tiles[t][edge:edge + 1, :], (SUBLANE, LANE))
    return jnp.concatenate(tiles, axis=0)


def _inproj_kernel(x_ref, mod_ref, g_ref, w_ref, wlr_ref, wal_ref, bal_ref, *out_refs, segs, chunk):
    u = _modulate(x_ref[0], g_ref[...], mod_ref, 3).astype(BF16)
    lr = _dot(u, wlr_ref[...]).astype(BF16)
    z = _dot(lr, wal_ref[...]) + bal_ref[...]
    log_sig = jnp.minimum(z, 0.0) - jnp.log1p(jnp.exp(-jnp.abs(z)))
    g = log_sig * (1.0 / GLA_TAU)
    tm, n = g.shape
    for c0 in range(0, n, LANE):
        for r0 in range(0, tm, chunk):
            out_refs[-1][0, r0:r0 + chunk, c0:c0 + LANE] = _scan_rows(g[r0:r0 + chunk, c0:c0 + LANE], c0 < n // 2)
    off = 0
    for (kind, width), o_ref in zip(segs, out_refs[:-1]):
        if kind == "glu":
            a = _dot(u, w_ref[:, off:off + width])
            b = _dot(u, w_ref[:, off + width:off + 2 * width])
            o_ref[0] = (a * jax.nn.sigmoid(b)).astype(o_ref.dtype)
            off += 2 * width
        else:
            p = _dot(u, w_ref[:, off:off + width])
            if kind == "sigmoid":
                p = jax.nn.sigmoid(p)
            o_ref[0] = p.astype(o_ref.dtype)
            off += width


def _inproj_call(x, mod, mod_per_batch, g, w, wlr, wal, bal, segs, tm, chunk, name):
    bsz, s, d = x.shape
    assert tm % chunk == 0 and chunk % SUBLANE == 0
    mod_idx = (lambda b, j: (b, 0, 0)) if mod_per_batch else (lambda b, j: (0, 0, 0))
    n_dec = wal.shape[1]
    tok = lambda n: pl.BlockSpec((1, tm, n), lambda b, j: (b, j, 0))
    out_shape = [jax.ShapeDtypeStruct((bsz, s, width), BF16) for _, width in segs]
    out_shape.append(jax.ShapeDtypeStruct((bsz, s, n_dec), F32))
    out_specs = [tok(width) for _, width in segs] + [tok(n_dec)]
    return pl.pallas_call(
        functools.partial(_inproj_kernel, segs=segs, chunk=chunk),
        grid=(bsz, s // tm),
        in_specs=[tok(d),
                  pl.BlockSpec((1, N_MOD, d), mod_idx),
                  _const_spec((1, d)),
                  _const_spec(w.shape),
                  _const_spec(wlr.shape),
                  _const_spec(wal.shape),
                  _const_spec((1, n_dec))],
        out_specs=out_specs,
        out_shape=out_shape,
        compiler_params=_params(("parallel", "parallel")),
        name=name,
    )(x, mod, g.reshape(1, d), w, wlr, wal, bal.reshape(1, n_dec))


def _chunk_rows(i, c):
    return pl.ds(i * c, c) if isinstance(i, int) else pl.ds(pl.multiple_of(i * c, c), c)


def _gla_kernel(q_ref, k_ref, v_ref, bf_ref, bb_ref, kc_ref, vc_ref, bfc_ref, bbc_ref, o_ref,
                qm_s, km_s, kl_s, att_s, qb_s, u_s, dec_s, st_s, *, chunk, scale):
    c = chunk
    dk = q_ref.shape[2]
    n = q_ref.shape[1] // c
    nc = kc_ref.shape[1] // c
    mid = c // 2
    row = lax.broadcasted_iota(jnp.int32, (c, c), 0)
    col = lax.broadcasted_iota(jnp.int32, (c, c), 1)
    lower = row >= col
    upper = row <= col

    uc, decc = [], []
    for i in range(nc):
        sl = slice(i * c, (i + 1) * c)
        bf, bb = bfc_ref[0, sl, :], bbc_ref[0, sl, :]
        blf, blb = bf[c - 1:c, :], bb[0:1, :]
        kcf = kc_ref[0, sl, :].astype(F32)
        kl = jnp.concatenate([(kcf * jnp.exp(blf - bf)).astype(BF16),
                              (kcf * jnp.exp(blb - bb)).astype(BF16)], axis=1)
        uc.append(_dot_t(vc_ref[0, sl, :], kl, 0, 0))
        decc.append((jnp.exp(blf), jnp.exp(blb)))
    stf0 = jnp.zeros((uc[0].shape[0], dk), F32)
    stb0 = stf0
    for i in range(nc):
        j = nc - 1 - i
        stf0 = stf0 * decc[i][0] + uc[i][:, :dk]
        stb0 = stb0 * decc[j][1] + uc[j][:, dk:]

    def factors(i):
        sl = _chunk_rows(i, c)
        bf, bb = bf_ref[0, sl, :], bb_ref[0, sl, :]
        blf, blb = bf[c - 1:c, :], bb[0:1, :]
        bmf, bmb = bf[mid:mid + 1, :], bb[mid:mid + 1, :]
        q = q_ref[0, sl, :].astype(F32) * scale
        k = k_ref[0, sl, :].astype(F32)
        qmf = q * jnp.exp(bf - bmf)
        qmb = q * jnp.exp(bb - bmb)
        kmf = k * jnp.exp(bmf - bf)
        kmb = k * jnp.exp(bmb - bb)
        qm_s[sl, :] = jnp.concatenate([qmf.astype(BF16), qmb.astype(BF16)], axis=1)
        km_s[sl, :] = jnp.concatenate([kmf.astype(BF16), kmb.astype(BF16)], axis=1)
        qb_s[sl, :] = jnp.concatenate([(qmf * jnp.exp(bmf)).astype(BF16),
                                       (qmb * jnp.exp(bmb)).astype(BF16)], axis=1)
        kl_s[sl, :] = jnp.concatenate([(kmf * jnp.exp(blf - bmf)).astype(BF16),
                                       (kmb * jnp.exp(blb - bmb)).astype(BF16)], axis=1)
        dec_s[i] = jnp.concatenate([jnp.exp(blf), jnp.exp(blb)], axis=1)

    def products(i):
        sl = _chunk_rows(i, c)
        qm, km = qm_s[sl, :], km_s[sl, :]
        att = (jnp.where(lower, _dot_t(qm[:, 0:dk], km[:, 0:dk], 1, 1), 0.0)
               + jnp.where(upper, _dot_t(qm[:, dk:2 * dk], km[:, dk:2 * dk], 1, 1), 0.0))
        att_s[sl, :] = att.astype(BF16)
        u_s[i] = _dot_t(v_ref[0, sl, :], kl_s[sl, :], 0, 0)

    def prep(i, carry):
        products(i - 1)
        factors(i)
        return carry

    factors(0)
    lax.fori_loop(1, n, prep, 0, unroll=3)
    products(n - 1)

    def scan(i, carry):
        stf, stb = carry
        j = n - 1 - i
        st_s[i, :, 0:dk] = stf.astype(BF16)
        st_s[j, :, dk:2 * dk] = stb.astype(BF16)
        stf = stf * dec_s[i][:, 0:dk] + u_s[i, :, 0:dk]
        stb = stb * dec_s[j][:, dk:2 * dk] + u_s[j, :, dk:2 * dk]
        return stf, stb

    lax.fori_loop(0, n, scan, (stf0, stb0))

    def emit(i, carry):
        sl = _chunk_rows(i, c)
        o_ref[0, sl, :] = _dot(att_s[sl, :], v_ref[0, sl, :]) + _dot_t(qb_s[sl, :], st_s[i], 1, 1)
        return carry

    lax.fori_loop(0, n, emit, 0, unroll=4)


def _gla_call(q, k, v, gd, kc, vc, gdc, chunk):
    bsz, s, qk_w = q.shape
    v_w = v.shape[2]
    sc = kc.shape[1]
    dk, dv = qk_w // GLA_HEADS, v_w // GLA_HEADS
    n = s // chunk
    assert s % chunk == 0 and sc % chunk == 0 and dk == LANE
    hd = lambda rows, w, shift=0: pl.BlockSpec((1, rows, w), lambda b, h: (b, 0, h + shift))
    return pl.pallas_call(
        functools.partial(_gla_kernel, chunk=chunk, scale=dk ** -0.5),
        grid=(bsz, GLA_HEADS),
        in_specs=[hd(s, dk), hd(s, dk), hd(s, dv), hd(s, dk), hd(s, dk, GLA_HEADS),
                  hd(sc, dk), hd(sc, dv), hd(sc, dk), hd(sc, dk, GLA_HEADS)],
        out_specs=hd(s, dv),
        out_shape=jax.ShapeDtypeStruct((bsz, s, v_w), F32),
        scratch_shapes=[pltpu.VMEM((s, 2 * dk), BF16),
                        pltpu.VMEM((s, 2 * dk), BF16),
                        pltpu.VMEM((s, 2 * dk), BF16),
                        pltpu.VMEM((s, chunk), BF16),
                        pltpu.VMEM((s, 2 * dk), BF16),
                        pltpu.VMEM((n, dv, 2 * dk), F32),
                        pltpu.VMEM((n, 1, 2 * dk), F32),
                        pltpu.VMEM((n, dv, 2 * dk), BF16)],
        compiler_params=_params(("parallel", "parallel")),
        name="gla",
    )(q, k, v, gd, gd, kc, vc, gdc, gdc)


def _conv_geometry(width, halo):
    base = halo - width // 2
    n_a = -(-(width + base) // SUBLANE)
    return base, n_a, SUBLANE * (n_a - 1)


def _shift_selector(block, extra):
    span = block + extra
    rows = jnp.arange(SUBLANE * span)
    src = rows % span + rows // span
    return (jnp.arange(span + SUBLANE)[None, :] == src[:, None]).astype(BF16)


def _depthwise_conv(win_ref, sel_ref, sh_ref, dww_ref, out_ref, halo, row_block):
    tm, dc = out_ref.shape
    width = dww_ref.shape[0]
    base, n_a, extra = _conv_geometry(width, halo)
    span = sel_ref.shape[0] // SUBLANE
    block = span - extra
    for bi, r0 in enumerate(range(0, tm, block)):
        sh = sh_ref.at[bi % sh_ref.shape[0]]
        sh[...] = _dot(sel_ref[...], win_ref[r0:r0 + span + SUBLANE, :])
        for c0 in range(0, dc, LANE):
            w = dww_ref[:, c0:c0 + LANE]
            for q0 in range(0, block, row_block):
                acc = None
                for phase in range(SUBLANE):
                    for a in range(n_a):
                        d = SUBLANE * a + phase - base
                        if not 0 <= d < width:
                            continue
                        lo = phase * span + SUBLANE * a + q0
                        term = sh[lo:lo + row_block, c0:c0 + LANE] * w[d:d + 1, :]
                        acc = term if acc is None else acc + term
                out_ref[r0 + q0:r0 + q0 + row_block, c0:c0 + LANE] = acc


def _conv_branch(zc_ref, sel_ref, j, nj, tm, halo, win_ref, sh_ref, conv_ref, dww_ref, dwb_ref, lng_ref, lnb_ref):
    seq = zc_ref.shape[1]
    t0 = pl.multiple_of(j * tm, tm)
    prev_lo = pl.multiple_of(jnp.maximum(t0 - halo, 0), halo)
    next_lo = pl.multiple_of(jnp.minimum(t0 + tm, seq - halo), halo)
    prev = zc_ref[0, pl.ds(prev_lo, halo), :]
    nxt = zc_ref[0, pl.ds(next_lo, halo), :]
    win_ref[0:halo, :] = jnp.where(j > 0, prev, jnp.zeros_like(prev))
    win_ref[halo:halo + tm, :] = zc_ref[0, pl.ds(t0, tm), :]
    win_ref[halo + tm:, :] = jnp.where(j < nj - 1, nxt, jnp.zeros_like(nxt))

    _depthwise_conv(win_ref, sel_ref, sh_ref, dww_ref, conv_ref, halo, 8 * SUBLANE)
    z = conv_ref[...] + dwb_ref[...]
    mu = jnp.mean(z, axis=-1, keepdims=True)
    zc = z - mu
    var = jnp.mean(zc * zc, axis=-1, keepdims=True)
    z = zc * lax.rsqrt(var + EPS) * lng_ref[...] + lnb_ref[...]
    return _silu(z).astype(BF16)


def _merge_kernel(zc_ref, sel_ref, o_ref, og_ref, sga_ref, sgb_ref, x_ref, mod_ref, dww_ref, dwb_ref, lng_ref,
                  lnb_ref, gng_ref, wco_ref, wgo_ref, wo_ref, out_ref, win_ref, sh_ref, conv_ref, *, halo):
    tm = x_ref.shape[1]
    act = _conv_branch(zc_ref, sel_ref, pl.program_id(1), pl.num_programs(1), tm, halo, win_ref, sh_ref, conv_ref,
                       dww_ref, dwb_ref, lng_ref, lnb_ref)
    y_conv = _dot(act, wco_ref[...])

    o = o_ref[0]
    dv = o.shape[1] // GLA_HEADS
    parts = []
    for h in range(GLA_HEADS):
        oh = o[:, h * dv:(h + 1) * dv]
        parts.append(oh * lax.rsqrt(jnp.mean(oh * oh, axis=-1, keepdims=True) + EPS))
    on = jnp.concatenate(parts, axis=1) * gng_ref[...]
    y_gla = _dot((on * _silu(og_ref[0].astype(F32))).astype(BF16), wgo_ref[...])

    merged = sga_ref[0].astype(F32) * y_conv + sgb_ref[0].astype(F32) * y_gla
    mix = _dot(merged.astype(BF16), wo_ref[...])
    out_ref[0] = x_ref[0] + mod_ref[0, 5:6, :] * mix


def _merge_call(zc, o, og, sga, sgb, x1, mod, dw_w, dw_b, ln_g, ln_b, gn_g, wco, wgo, wo, tm):
    bsz, s, d = x1.shape
    dc = zc.shape[2]
    halo = 2 * SUBLANE
    block = LANE
    _, _, extra = _conv_geometry(dw_w.shape[0], halo)
    assert dw_w.shape[0] // 2 <= halo and s % tm == 0 and tm % block == 0 and extra + SUBLANE <= 2 * halo
    sel = _shift_selector(block, extra)
    tok = lambda n: pl.BlockSpec((1, tm, n), lambda b, j: (b, j, 0))
    row = lambda a: a.reshape(1, -1)
    return pl.pallas_call(
        functools.partial(_merge_kernel, halo=halo),
        grid=(bsz, s // tm),
        in_specs=[pl.BlockSpec((1, s, dc), lambda b, j: (b, 0, 0)),
                  _const_spec(sel.shape),
                  tok(o.shape[2]), tok(og.shape[2]), tok(d), tok(d), tok(d),
                  pl.BlockSpec((1, N_MOD, d), lambda b, j: (b, 0, 0)),
                  _const_spec(dw_w.shape), _const_spec((1, dc)), _const_spec((1, dc)), _const_spec((1, dc)),
                  _const_spec((1, o.shape[2])),
                  _const_spec(wco.shape), _const_spec(wgo.shape), _const_spec(wo.shape)],
        out_specs=tok(d),
        out_shape=jax.ShapeDtypeStruct((bsz, s, d), F32),
        scratch_shapes=[pltpu.VMEM((tm + 2 * halo, dc), zc.dtype),
                        pltpu.VMEM((2,) + (sel.shape[0], dc), F32),
                        pltpu.VMEM((tm, dc), F32)],
        compiler_params=_params(("parallel", "arbitrary")),
        name="merge",
    )(zc, sel, o, og, sga, sgb, x1, mod, dw_w, row(dw_b), row(ln_g), row(ln_b), row(gn_g), wco, wgo, wo)


def kernel(x, c, ctx, c_ctx, w_mod, b_mod, g_ffn1, w1_gu, w1_down, g_mix, w_in, dw_weight, dw_bias,
           conv_ln_g, conv_ln_b, w_conv_out, w_alpha_f, b_alpha_f, w_alpha_b, b_alpha_b, gla_norm_g,
           w_gla_out, w_out, g_ffn2, w2_gu, w2_down, g_final):
    depth = w_mod.shape[0]
    assert depth == 1, "context-side mixing of non-final layers is not implemented"
    bsz, seq, d = x.shape
    d_conv = dw_weight.shape[2]
    qk_w = w_alpha_f.shape[2]
    v_w = gla_norm_g.shape[1]
    r = GLA_LOWRANK

    pad_rows = (-(bsz + 1)) % SUBLANE
    c_all = jnp.concatenate([c, c_ctx[None, :], jnp.zeros((pad_rows, d), F32)], axis=0)
    mod = _mod_call(c_all, w_mod[0], b_mod[0])
    mod_x = mod[:bsz].reshape(bsz, N_MOD, d)
    mod_c = mod[bsz:bsz + 1].reshape(1, N_MOD, d)

    w1gu, w1d = w1_gu[0].astype(BF16), w1_down[0].astype(BF16)
    w2gu, w2d = w2_gu[0].astype(BF16), w2_down[0].astype(BF16)
    wco, wgo, wo = w_conv_out[0].astype(BF16), w_gla_out[0].astype(BF16), w_out[0].astype(BF16)
    wi = w_in[0]
    o_q = 2 * d_conv
    o_k, o_v = o_q + qk_w, o_q + 2 * qk_w
    o_og = o_v + v_w
    o_af = o_og + v_w
    o_ga = o_af + 2 * r
    w_x = jnp.concatenate([wi[:, :o_af], wi[:, o_ga:]], axis=1).astype(BF16)
    w_c = wi[:, o_k:o_og].astype(BF16)
    wlr = jnp.concatenate([wi[:, o_af:o_ga], jnp.zeros((d, LANE - 2 * r), F32)], axis=1).astype(BF16)
    wal = jnp.zeros((LANE, 2 * qk_w), F32)
    wal = wal.at[:r, :qk_w].set(w_alpha_f[0]).at[r:2 * r, qk_w:].set(w_alpha_b[0]).astype(BF16)
    bal = jnp.concatenate([b_alpha_f[0], b_alpha_b[0]])

    x1 = _ffn_call(x, mod_x, True, g_ffn1[0], w1gu, w1d, 0, 512, name="ffn1_x")
    h1 = _ffn_call(ctx, mod_c, False, g_ffn1[0], w1gu, w1d, 0, ctx.shape[1], name="ffn1_ctx")

    segs_x = (("glu", d_conv), ("plain", qk_w), ("plain", qk_w), ("plain", v_w), ("plain", v_w),
              ("sigmoid", d), ("sigmoid", d))
    zc, q, k, v, og, sga, sgb, gd = _inproj_call(x1, mod_x, True, g_mix[0], w_x, wlr, wal, bal,
                                                 segs_x, 512, GLA_CHUNK, "inproj_x")
    segs_c = (("plain", qk_w), ("plain", v_w))
    kc, vc, gdc = _inproj_call(h1, mod_c, False, g_mix[0], w_c, wlr, wal, bal, segs_c,
                               ctx.shape[1], GLA_CHUNK, "inproj_ctx")

    o = _gla_call(q, k, v, gd, kc, vc, gdc, GLA_CHUNK)

    x2 = _merge_call(zc, o, og, sga, sgb, x1, mod_x, dw_weight[0], dw_bias[0], conv_ln_g[0],
                     conv_ln_b[0], gla_norm_g[0], wco, wgo, wo, 256)

    return _ffn_call(x2, mod_x, True, g_ffn2[0], w2gu, w2d, 6, 512, "ffn2_x", g_final=g_final)
```

```python
import functools

import jax
import jax.numpy as jnp
from jax import lax
from jax.experimental import pallas as pl
from jax.experimental.pallas import tpu as pltpu

F32 = jnp.float32
BF16 = jnp.bfloat16

EPS = 1e-6
N_MOD = 9
GLA_HEADS = 4
GLA_TAU = 16.0
GLA_LOWRANK = 16
GLA_CHUNK = 128
LANE = 128
SUBLANE = 8
MXU_K = 256
FFN_CHUNKS = 2
VMEM_LIMIT = 56 * 1024 * 1024


def _dot(a, b):
    return jnp.dot(a, b, preferred_element_type=F32)


def _dot_t(a, b, ca, cb):
    return lax.dot_general(a, b, (((ca,), (cb,)), ((), ())), preferred_element_type=F32)


def _silu(x):
    return x * jax.nn.sigmoid(x)


def _rms(h, g):
    return h * lax.rsqrt(jnp.mean(h * h, axis=-1, keepdims=True) + EPS) * g


def _modulate(h, g, mod_ref, row):
    shift = mod_ref[0, row:row + 1, :]
    scale = mod_ref[0, row + 1:row + 2, :]
    return _rms(h, g) * (1.0 + scale) + shift


def _const_spec(shape):
    nd = len(shape)
    return pl.BlockSpec(shape, lambda *_: (0,) * nd, pipeline_mode=pl.Buffered(1))


def _params(sem, flags=None):
    return pltpu.CompilerParams(dimension_semantics=sem, vmem_limit_bytes=VMEM_LIMIT, flags=flags)


def _mod_kernel(c_ref, w_ref, b_ref, o_ref):
    s = _silu(c_ref[...]).astype(BF16)
    o_ref[...] = _dot(s, w_ref[...].astype(BF16)) + b_ref[...]


def _mod_call(c_all, w_mod, b_mod, n_steps=8):
    rows, d = c_all.shape
    n = w_mod.shape[1]
    bn = n // n_steps
    return pl.pallas_call(
        _mod_kernel,
        grid=(n_steps,),
        in_specs=[pl.BlockSpec((rows, d), lambda j: (0, 0)),
                  pl.BlockSpec((d, bn), lambda j: (0, j)),
                  pl.BlockSpec((1, bn), lambda j: (0, j))],
        out_specs=pl.BlockSpec((rows, bn), lambda j: (0, j)),
        out_shape=jax.ShapeDtypeStruct((rows, n), F32),
        compiler_params=_params(("arbitrary",)),
        name="mod",
    )(c_all, w_mod, b_mod.reshape(1, n))


def _ffn_bounds(f, n_chunks):
    tiles = -(-f // MXU_K)
    edges = [min(f, MXU_K * (-(-tiles * j // n_chunks))) for j in range(n_chunks + 1)]
    return list(zip(edges[:-1], edges[1:]))


def _half_ffn(h, mod_ref, g_ref, wgu_ref, wd_ref, mod_row, n_chunks):
    u = _modulate(h, g_ref[...], mod_ref, mod_row).astype(BF16)
    f = wd_ref.shape[0]
    acc = None
    for lo, hi in _ffn_bounds(f, n_chunks):
        a = _dot(u, wgu_ref[:, lo:hi])
        b = _dot(u, wgu_ref[:, f + lo:f + hi])
        m = (_silu(a) * b).astype(BF16)
        d = _dot(m, wd_ref[lo:hi, :])
        acc = d if acc is None else acc + d
    gate = mod_ref[0, mod_row + 2:mod_row + 3, :]
    return h + (0.5 * gate) * acc


def _ffn_kernel(h_ref, mod_ref, g_ref, wgu_ref, wd_ref, *rest, mod_row, n_chunks):
    gfin_ref, o_ref = rest if len(rest) == 2 else (None, rest[0])
    y = _half_ffn(h_ref[0], mod_ref, g_ref, wgu_ref, wd_ref, mod_row, n_chunks)
    o_ref[0] = y if gfin_ref is None else _rms(y, gfin_ref[...])


def _ffn_call(h, mod, mod_per_batch, g, wgu, wd, mod_row, tm, name, g_final=None):
    bsz, s, d = h.shape
    f = wd.shape[0]
    mod_idx = (lambda b, j: (b, 0, 0)) if mod_per_batch else (lambda b, j: (0, 0, 0))
    in_specs = [pl.BlockSpec((1, tm, d), lambda b, j: (b, j, 0)),
                pl.BlockSpec((1, N_MOD, d), mod_idx),
                _const_spec((1, d)),
                _const_spec((d, 2 * f)),
                _const_spec((f, d))]
    args = [h, mod, g.reshape(1, d), wgu, wd]
    if g_final is not None:
        in_specs.append(_const_spec((1, d)))
        args.append(g_final.reshape(1, d))
    return pl.pallas_call(
        functools.partial(_ffn_kernel, mod_row=mod_row, n_chunks=FFN_CHUNKS),
        grid=(bsz, s // tm),
        in_specs=in_specs,
        out_specs=pl.BlockSpec((1, tm, d), lambda b, j: (b, j, 0)),
        out_shape=jax.ShapeDtypeStruct((bsz, s, d), F32),
        compiler_params=_params(("parallel", "parallel")),
        name=name,
    )(*args)


def _scan_rows(g, forward):
    n_tiles = g.shape[0] // SUBLANE
    rid = lax.broadcasted_iota(jnp.int32, (SUBLANE, LANE), 0)
    tiles = [g[SUBLANE * t:SUBLANE * (t + 1), :] for t in range(n_tiles)]
    for d in [1 << s for s in range(SUBLANE.bit_length() - 1)]:
        if forward:
            tiles = [x + jnp.where(rid >= d, pltpu.roll(x, d, axis=0), 0.0) for x in tiles]
        else:
            tiles = [x + jnp.where(rid < SUBLANE - d, pltpu.roll(x, SUBLANE - d, axis=0), 0.0) for x in tiles]
    order = range(n_tiles) if forward else range(n_tiles - 1, -1, -1)
    edge = SUBLANE - 1 if forward else 0
    carry = None
    for t in order:
        if carry is not None:
            tiles[t] = tiles[t] + carry
        carry = jnp.broadcast_to(tiles[t][edge:edge + 1, :], (SUBLANE, LANE))
    return jnp.concatenate(tiles, axis=0)


def _inproj_kernel(x_ref, mod_ref, g_ref, w_ref, wlr_ref, wal_ref, bal_ref, *out_refs, segs, chunk):
    u = _modulate(x_ref[0], g_ref[...], mod_ref, 3).astype(BF16)
    lr = _dot(u, wlr_ref[...]).astype(BF16)
    z = _dot(lr, wal_ref[...]) + bal_ref[...]
    log_sig = jnp.minimum(z, 0.0) - jnp.log1p(jnp.exp(-jnp.abs(z)))
    g = log_sig * (1.0 / GLA_TAU)
    tm, n = g.shape
    for c0 in range(0, n, LANE):
        for r0 in range(0, tm, chunk):
            out_refs[-1][0, c0 // LANE, r0:r0 + chunk, :] = _scan_rows(g[r0:r0 + chunk, c0:c0 + LANE], c0 < n // 2)
    off = 0
    for (kind, width), o_ref in zip(segs, out_refs[:-1]):
        if kind == "glu":
            a = _dot(u, w_ref[:, off:off + width])
            b = _dot(u, w_ref[:, off + width:off + 2 * width])
            o_ref[0] = (a * jax.nn.sigmoid(b)).astype(o_ref.dtype)
            off += 2 * width
        else:
            p = _dot(u, w_ref[:, off:off + width])
            if kind == "sigmoid":
                p = jax.nn.sigmoid(p)
            if kind == "heads":
                hw = width // GLA_HEADS
                for h in range(GLA_HEADS):
                    o_ref[0, h] = p[:, h * hw:(h + 1) * hw].astype(o_ref.dtype)
            else:
                o_ref[0] = p.astype(o_ref.dtype)
            off += width


def _inproj_call(x, mod, mod_per_batch, g, w, wlr, wal, bal, segs, tm, chunk, name):
    bsz, s, d = x.shape
    assert tm % chunk == 0 and chunk % SUBLANE == 0
    mod_idx = (lambda b, j: (b, 0, 0)) if mod_per_batch else (lambda b, j: (0, 0, 0))
    n_dec = wal.shape[1]
    tok = lambda n: pl.BlockSpec((1, tm, n), lambda b, j: (b, j, 0))
    heads = lambda nh, w: pl.BlockSpec((1, nh, tm, w), lambda b, j: (b, 0, j, 0))
    out_shape, out_specs = [], []
    for kind, width in segs:
        if kind == "heads":
            out_shape.append(jax.ShapeDtypeStruct((bsz, GLA_HEADS, s, width // GLA_HEADS), BF16))
            out_specs.append(heads(GLA_HEADS, width // GLA_HEADS))
        else:
            out_shape.append(jax.ShapeDtypeStruct((bsz, s, width), BF16))
            out_specs.append(tok(width))
    out_shape.append(jax.ShapeDtypeStruct((bsz, n_dec // LANE, s, LANE), F32))
    out_specs.append(heads(n_dec // LANE, LANE))
    return pl.pallas_call(
        functools.partial(_inproj_kernel, segs=segs, chunk=chunk),
        grid=(bsz, s // tm),
        in_specs=[tok(d),
                  pl.BlockSpec((1, N_MOD, d), mod_idx),
                  _const_spec((1, d)),
                  _const_spec(w.shape),
                  _const_spec(wlr.shape),
                  _const_spec(wal.shape),
                  _const_spec((1, n_dec))],
        out_specs=out_specs,
        out_shape=out_shape,
        compiler_params=_params(("parallel", "parallel")),
        name=name,
    )(x, mod, g.reshape(1, d), w, wlr, wal, bal.reshape(1, n_dec))


def _chunk_rows(i, c):
    return pl.ds(i * c, c) if isinstance(i, int) else pl.ds(pl.multiple_of(i * c, c), c)


def _gla_kernel(q_ref, k_ref, v_ref, bf_ref, bb_ref, kc_ref, vc_ref, bfc_ref, bbc_ref, o_ref,
                qm_s, km_s, kl_s, att_s, qb_s, u_s, dec_s, st_s, *, chunk, scale):
    c = chunk
    dk = q_ref.shape[1]
    n = q_ref.shape[0] // c
    nc = kc_ref.shape[0] // c
    mid = c // 2
    row = lax.broadcasted_iota(jnp.int32, (c, c), 0)
    col = lax.broadcasted_iota(jnp.int32, (c, c), 1)
    lower = row >= col
    upper = row <= col

    uc, decc = [], []
    for i in range(nc):
        sl = slice(i * c, (i + 1) * c)
        bf, bb = bfc_ref[sl, :], bbc_ref[sl, :]
        blf, blb = bf[c - 1:c, :], bb[0:1, :]
        kcf = kc_ref[sl, :].astype(F32)
        kl = jnp.concatenate([(kcf * jnp.exp(blf - bf)).astype(BF16),
                              (kcf * jnp.exp(blb - bb)).astype(BF16)], axis=1)
        uc.append(_dot_t(vc_ref[sl, :], kl, 0, 0))
        decc.append((jnp.exp(blf), jnp.exp(blb)))
    stf0 = jnp.zeros((uc[0].shape[0], dk), F32)
    stb0 = stf0
    for i in range(nc):
        j = nc - 1 - i
        stf0 = stf0 * decc[i][0] + uc[i][:, :dk]
        stb0 = stb0 * decc[j][1] + uc[j][:, dk:]

    def factors(i):
        sl = _chunk_rows(i, c)
        bf, bb = bf_ref[sl, :], bb_ref[sl, :]
        blf, blb = bf[c - 1:c, :], bb[0:1, :]
        bmf, bmb = bf[mid:mid + 1, :], bb[mid:mid + 1, :]
        q = q_ref[sl, :].astype(F32) * scale
        k = k_ref[sl, :].astype(F32)
        qmf = q * jnp.exp(bf - bmf)
        qmb = q * jnp.exp(bb - bmb)
        kmf = k * jnp.exp(bmf - bf)
        kmb = k * jnp.exp(bmb - bb)
        qm_s[sl, :] = jnp.concatenate([qmf.astype(BF16), qmb.astype(BF16)], axis=1)
        km_s[sl, :] = jnp.concatenate([kmf.astype(BF16), kmb.astype(BF16)], axis=1)
        qb_s[sl, :] = jnp.concatenate([(qmf * jnp.exp(bmf)).astype(BF16),
                                       (qmb * jnp.exp(bmb)).astype(BF16)], axis=1)
        kl_s[sl, :] = jnp.concatenate([(kmf * jnp.exp(blf - bmf)).astype(BF16),
                                       (kmb * jnp.exp(blb - bmb)).astype(BF16)], axis=1)
        dec_s[i] = jnp.concatenate([jnp.exp(blf), jnp.exp(blb)], axis=1)

    def products(i):
        sl = _chunk_rows(i, c)
        qm, km = qm_s[sl, :], km_s[sl, :]
        att = (jnp.where(lower, _dot_t(qm[:, 0:dk], km[:, 0:dk], 1, 1), 0.0)
               + jnp.where(upper, _dot_t(qm[:, dk:2 * dk], km[:, dk:2 * dk], 1, 1), 0.0))
        att_s[sl, :] = att.astype(BF16)
        u_s[i] = _dot_t(v_ref[sl, :], kl_s[sl, :], 0, 0)

    def prep(i, carry):
        products(i - 1)
        factors(i)
        return carry

    factors(0)
    lax.fori_loop(1, n, prep, 0, unroll=3)
    products(n - 1)

    def scan(i, carry):
        stf, stb = carry
        j = n - 1 - i
        st_s[i, :, 0:dk] = stf.astype(BF16)
        st_s[j, :, dk:2 * dk] = stb.astype(BF16)
        stf = stf * dec_s[i][:, 0:dk] + u_s[i, :, 0:dk]
        stb = stb * dec_s[j][:, dk:2 * dk] + u_s[j, :, dk:2 * dk]
        return stf, stb

    lax.fori_loop(0, n, scan, (stf0, stb0))

    def emit(i, carry):
        sl = _chunk_rows(i, c)
        o_ref[sl, :] = _dot(att_s[sl, :], v_ref[sl, :]) + _dot_t(qb_s[sl, :], st_s[i], 1, 1)
        return carry

    lax.fori_loop(0, n, emit, 0, unroll=4)


def _gla_call(q, k, v, gd, kc, vc, gdc, chunk):
    bsz, nh, s, dk = q.shape
    dv = v.shape[3]
    sc = kc.shape[2]
    n = s // chunk
    assert s % chunk == 0 and sc % chunk == 0 and dk == LANE and gd.shape[1] == 2 * nh
    hd = lambda rows, w, shift=0: pl.BlockSpec((None, None, rows, w), lambda b, h: (b, h + shift, 0, 0))
    return pl.pallas_call(
        functools.partial(_gla_kernel, chunk=chunk, scale=dk ** -0.5),
        grid=(bsz, nh),
        in_specs=[hd(s, dk), hd(s, dk), hd(s, dv), hd(s, dk), hd(s, dk, nh),
                  hd(sc, dk), hd(sc, dv), hd(sc, dk), hd(sc, dk, nh)],
        out_specs=hd(s, dv),
        out_shape=jax.ShapeDtypeStruct((bsz, nh, s, dv), F32),
        scratch_shapes=[pltpu.VMEM((s, 2 * dk), BF16),
                        pltpu.VMEM((s, 2 * dk), BF16),
                        pltpu.VMEM((s, 2 * dk), BF16),
                        pltpu.VMEM((s, chunk), BF16),
                        pltpu.VMEM((s, 2 * dk), BF16),
                        pltpu.VMEM((n, dv, 2 * dk), F32),
                        pltpu.VMEM((n, 1, 2 * dk), F32),
                        pltpu.VMEM((n, dv, 2 * dk), BF16)],
        compiler_params=_params(("parallel", "parallel")),
        name="gla",
    )(q, k, v, gd, gd, kc, vc, gdc, gdc)


def _conv_geometry(width, halo):
    base = halo - width // 2
    n_a = -(-(width + base) // SUBLANE)
    return base, n_a, SUBLANE * (n_a - 1)


def _shift_selector(block, extra):
    span = block + extra
    rows = jnp.arange(SUBLANE * span)
    src = rows % span + rows // span
    return (jnp.arange(span + SUBLANE)[None, :] == src[:, None]).astype(BF16)


def _depthwise_conv(win_ref, sel_ref, sh_ref, dww_ref, out_ref, halo, row_block):
    tm, dc = out_ref.shape
    width = dww_ref.shape[0]
    base, n_a, extra = _conv_geometry(width, halo)
    span = sel_ref.shape[0] // SUBLANE
    block = span - extra
    for bi, r0 in enumerate(range(0, tm, block)):
        sh = sh_ref.at[bi % sh_ref.shape[0]]
        sh[...] = _dot(sel_ref[...], win_ref[r0:r0 + span + SUBLANE, :])
        for c0 in range(0, dc, LANE):
            w = dww_ref[:, c0:c0 + LANE]
            for q0 in range(0, block, row_block):
                acc = None
                for phase in range(SUBLANE):
                    for a in range(n_a):
                        d = SUBLANE * a + phase - base
                        if not 0 <= d < width:
                            continue
                        lo = phase * span + SUBLANE * a + q0
                        term = sh[lo:lo + row_block, c0:c0 + LANE] * w[d:d + 1, :]
                        acc = term if acc is None else acc + term
                out_ref[r0 + q0:r0 + q0 + row_block, c0:c0 + LANE] = acc


def _conv_branch(zc_ref, sel_ref, j, nj, tm, halo, win_ref, sh_ref, conv_ref, dww_ref, dwb_ref, lng_ref, lnb_ref):
    seq = zc_ref.shape[1]
    t0 = pl.multiple_of(j * tm, tm)
    prev_lo = pl.multiple_of(jnp.maximum(t0 - halo, 0), halo)
    next_lo = pl.multiple_of(jnp.minimum(t0 + tm, seq - halo), halo)
    prev = zc_ref[0, pl.ds(prev_lo, halo), :]
    nxt = zc_ref[0, pl.ds(next_lo, halo), :]
    win_ref[0:halo, :] = jnp.where(j > 0, prev, jnp.zeros_like(prev))
    win_ref[halo:halo + tm, :] = zc_ref[0, pl.ds(t0, tm), :]
    win_ref[halo + tm:, :] = jnp.where(j < nj - 1, nxt, jnp.zeros_like(nxt))

    _depthwise_conv(win_ref, sel_ref, sh_ref, dww_ref, conv_ref, halo, 8 * SUBLANE)
    z = conv_ref[...] + dwb_ref[...]
    mu = jnp.mean(z, axis=-1, keepdims=True)
    zc = z - mu
    var = jnp.mean(zc * zc, axis=-1, keepdims=True)
    z = zc * lax.rsqrt(var + EPS) * lng_ref[...] + lnb_ref[...]
    return _silu(z).astype(BF16)


def _merge_kernel(zc_ref, sel_ref, o_ref, og_ref, sga_ref, sgb_ref, x_ref, mod_ref, dww_ref, dwb_ref, lng_ref,
                  lnb_ref, gng_ref, wco_ref, wgo_ref, wo_ref, out_ref, win_ref, sh_ref, conv_ref, *, halo):
    tm = x_ref.shape[1]
    act = _conv_branch(zc_ref, sel_ref, pl.program_id(1), pl.num_programs(1), tm, halo, win_ref, sh_ref, conv_ref,
                       dww_ref, dwb_ref, lng_ref, lnb_ref)
    y_conv = _dot(act, wco_ref[...])

    parts = []
    for h in range(o_ref.shape[1]):
        oh = o_ref[0, h]
        parts.append(oh * lax.rsqrt(jnp.mean(oh * oh, axis=-1, keepdims=True) + EPS))
    on = jnp.concatenate(parts, axis=1) * gng_ref[...]
    y_gla = _dot((on * _silu(og_ref[0].astype(F32))).astype(BF16), wgo_ref[...])

    merged = sga_ref[0].astype(F32) * y_conv + sgb_ref[0].astype(F32) * y_gla
    mix = _dot(merged.astype(BF16), wo_ref[...])
    out_ref[0] = x_ref[0] + mod_ref[0, 5:6, :] * mix


def _merge_call(zc, o, og, sga, sgb, x1, mod, dw_w, dw_b, ln_g, ln_b, gn_g, wco, wgo, wo, tm):
    bsz, s, d = x1.shape
    dc = zc.shape[2]
    halo = 2 * SUBLANE
    block = LANE
    _, _, extra = _conv_geometry(dw_w.shape[0], halo)
    assert dw_w.shape[0] // 2 <= halo and s % tm == 0 and tm % block == 0 and extra + SUBLANE <= 2 * halo
    sel = _shift_selector(block, extra)
    tok = lambda n: pl.BlockSpec((1, tm, n), lambda b, j: (b, j, 0))
    row = lambda a: a.reshape(1, -1)
    return pl.pallas_call(
        functools.partial(_merge_kernel, halo=halo),
        grid=(bsz, s // tm),
        in_specs=[pl.BlockSpec((1, s, dc), lambda b, j: (b, 0, 0)),
                  _const_spec(sel.shape),
                  pl.BlockSpec((1,) + (o.shape[1], tm, o.shape[3]), lambda b, j: (b, 0, j, 0)),
                  tok(og.shape[2]), tok(d), tok(d), tok(d),
                  pl.BlockSpec((1, N_MOD, d), lambda b, j: (b, 0, 0)),
                  _const_spec(dw_w.shape), _const_spec((1, dc)), _const_spec((1, dc)), _const_spec((1, dc)),
                  _const_spec((1, og.shape[2])),
                  _const_spec(wco.shape), _const_spec(wgo.shape), _const_spec(wo.shape)],
        out_specs=tok(d),
        out_shape=jax.ShapeDtypeStruct((bsz, s, d), F32),
        scratch_shapes=[pltpu.VMEM((tm + 2 * halo, dc), zc.dtype),
                        pltpu.VMEM((2,) + (sel.shape[0], dc), F32),
                        pltpu.VMEM((tm, dc), F32)],
        compiler_params=_params(("parallel", "arbitrary")),
        name="merge",
    )(zc, sel, o, og, sga, sgb, x1, mod, dw_w, row(dw_b), row(ln_g), row(ln_b), row(gn_g), wco, wgo, wo)


def kernel(x, c, ctx, c_ctx, w_mod, b_mod, g_ffn1, w1_gu, w1_down, g_mix, w_in, dw_weight, dw_bias,
           conv_ln_g, conv_ln_b, w_conv_out, w_alpha_f, b_alpha_f, w_alpha_b, b_alpha_b, gla_norm_g,
           w_gla_out, w_out, g_ffn2, w2_gu, w2_down, g_final):
    depth = w_mod.shape[0]
    assert depth == 1, "context-side mixing of non-final layers is not implemented"
    bsz, seq, d = x.shape
    d_conv = dw_weight.shape[2]
    qk_w = w_alpha_f.shape[2]
    v_w = gla_norm_g.shape[1]
    r = GLA_LOWRANK

    pad_rows = (-(bsz + 1)) % SUBLANE
    c_all = jnp.concatenate([c, c_ctx[None, :], jnp.zeros((pad_rows, d), F32)], axis=0)
    mod = _mod_call(c_all, w_mod[0], b_mod[0])
    mod_x = mod[:bsz].reshape(bsz, N_MOD, d)
    mod_c = mod[bsz:bsz + 1].reshape(1, N_MOD, d)

    w1gu, w1d = w1_gu[0].astype(BF16), w1_down[0].astype(BF16)
    w2gu, w2d = w2_gu[0].astype(BF16), w2_down[0].astype(BF16)
    wco, wgo, wo = w_conv_out[0].astype(BF16), w_gla_out[0].astype(BF16), w_out[0].astype(BF16)
    wi = w_in[0]
    o_q = 2 * d_conv
    o_k, o_v = o_q + qk_w, o_q + 2 * qk_w
    o_og = o_v + v_w
    o_af = o_og + v_w
    o_ga = o_af + 2 * r
    w_x = jnp.concatenate([wi[:, :o_af], wi[:, o_ga:]], axis=1).astype(BF16)
    w_c = wi[:, o_k:o_og].astype(BF16)
    wlr = jnp.concatenate([wi[:, o_af:o_ga], jnp.zeros((d, LANE - 2 * r), F32)], axis=1).astype(BF16)
    wal = jnp.zeros((LANE, 2 * qk_w), F32)
    wal = wal.at[:r, :qk_w].set(w_alpha_f[0]).at[r:2 * r, qk_w:].set(w_alpha_b[0]).astype(BF16)
    bal = jnp.concatenate([b_alpha_f[0], b_alpha_b[0]])

    x1 = _ffn_call(x, mod_x, True, g_ffn1[0], w1gu, w1d, 0, 1024, name="ffn1_x")
    h1 = _ffn_call(ctx, mod_c, False, g_ffn1[0], w1gu, w1d, 0, ctx.shape[1], name="ffn1_ctx")

    segs_x = (("glu", d_conv), ("heads", qk_w), ("heads", qk_w), ("heads", v_w), ("plain", v_w),
              ("sigmoid", d), ("sigmoid", d))
    zc, q, k, v, og, sga, sgb, gd = _inproj_call(x1, mod_x, True, g_mix[0], w_x, wlr, wal, bal,
                                                 segs_x, 512, GLA_CHUNK, "inproj_x")
    segs_c = (("heads", qk_w), ("heads", v_w))
    kc, vc, gdc = _inproj_call(h1, mod_c, False, g_mix[0], w_c, wlr, wal, bal, segs_c,
                               ctx.shape[1], GLA_CHUNK, "inproj_ctx")

    o = _gla_call(q, k, v, gd, kc, vc, gdc, GLA_CHUNK)

    x2 = _merge_call(zc, o, og, sga, sgb, x1, mod_x, dw_weight[0], dw_bias[0], conv_ln_g[0],
                     conv_ln_b[0], gla_norm_g[0], wco, wgo, wo, 256)

    return _ffn_call(x2, mod_x, True, g_ffn2[0], w2gu, w2d, 6, 1024, "ffn2_x", g_final=g_final)
```

```python
import functools

import jax
import jax.numpy as jnp
from jax import lax
from jax.experimental import pallas as pl
from jax.experimental.pallas import tpu as pltpu

F32 = jnp.float32
BF16 = jnp.bfloat16

EPS = 1e-6
N_MOD = 9
GLA_HEADS = 4
GLA_TAU = 16.0
GLA_LOWRANK = 16
GLA_CHUNK = 128
GLA_HEADS_PER_STEP = 2
LANE = 128
SUBLANE = 8
MXU_K = 256
FFN_CHUNKS = 2
VMEM_LIMIT = 56 * 1024 * 1024


def _dot(a, b):
    return jnp.dot(a, b, preferred_element_type=F32)


def _dot_t(a, b, ca, cb):
    return lax.dot_general(a, b, (((ca,), (cb,)), ((), ())), preferred_element_type=F32)


def _silu(x):
    return x * jax.nn.sigmoid(x)


def _rms(h, g):
    return h * lax.rsqrt(jnp.mean(h * h, axis=-1, keepdims=True) + EPS) * g


def _modulate(h, g, mod_ref, row):
    shift = mod_ref[0, row:row + 1, :]
    scale = mod_ref[0, row + 1:row + 2, :]
    return _rms(h, g) * (1.0 + scale) + shift


def _const_spec(shape):
    nd = len(shape)
    return pl.BlockSpec(shape, lambda *_: (0,) * nd, pipeline_mode=pl.Buffered(1))


def _params(sem, flags=None):
    return pltpu.CompilerParams(dimension_semantics=sem, vmem_limit_bytes=VMEM_LIMIT, flags=flags)


def _mod_kernel(c_ref, w_ref, b_ref, o_ref):
    s = _silu(c_ref[...]).astype(BF16)
    o_ref[...] = _dot(s, w_ref[...].astype(BF16)) + b_ref[...]


def _mod_call(c_all, w_mod, b_mod, n_steps=8):
    rows, d = c_all.shape
    n = w_mod.shape[1]
    bn = n // n_steps
    return pl.pallas_call(
        _mod_kernel,
        grid=(n_steps,),
        in_specs=[pl.BlockSpec((rows, d), lambda j: (0, 0)),
                  pl.BlockSpec((d, bn), lambda j: (0, j)),
                  pl.BlockSpec((1, bn), lambda j: (0, j))],
        out_specs=pl.BlockSpec((rows, bn), lambda j: (0, j)),
        out_shape=jax.ShapeDtypeStruct((rows, n), F32),
        compiler_params=_params(("arbitrary",)),
        name="mod",
    )(c_all, w_mod, b_mod.reshape(1, n))


def _ffn_bounds(f, n_chunks):
    tiles = -(-f // MXU_K)
    edges = [min(f, MXU_K * (-(-tiles * j // n_chunks))) for j in range(n_chunks + 1)]
    return list(zip(edges[:-1], edges[1:]))


def _half_ffn(h, mod_ref, g_ref, wgu_ref, wd_ref, mod_row, n_chunks):
    u = _modulate(h, g_ref[...], mod_ref, mod_row).astype(BF16)
    f = wd_ref.shape[0]
    acc = None
    for lo, hi in _ffn_bounds(f, n_chunks):
        a = _dot(u, wgu_ref[:, lo:hi])
        b = _dot(u, wgu_ref[:, f + lo:f + hi])
        m = (_silu(a) * b).astype(BF16)
        d = _dot(m, wd_ref[lo:hi, :])
        acc = d if acc is None else acc + d
    gate = mod_ref[0, mod_row + 2:mod_row + 3, :]
    return h + (0.5 * gate) * acc


def _ffn_kernel(h_ref, mod_ref, g_ref, wgu_ref, wd_ref, *rest, mod_row, n_chunks):
    gfin_ref, o_ref = rest if len(rest) == 2 else (None, rest[0])
    y = _half_ffn(h_ref[0], mod_ref, g_ref, wgu_ref, wd_ref, mod_row, n_chunks)
    o_ref[0] = y if gfin_ref is None else _rms(y, gfin_ref[...])


def _ffn_call(h, mod, mod_per_batch, g, wgu, wd, mod_row, tm, name, g_final=None):
    bsz, s, d = h.shape
    f = wd.shape[0]
    mod_idx = (lambda b, j: (b, 0, 0)) if mod_per_batch else (lambda b, j: (0, 0, 0))
    in_specs = [pl.BlockSpec((1, tm, d), lambda b, j: (b, j, 0)),
                pl.BlockSpec((1, N_MOD, d), mod_idx),
                _const_spec((1, d)),
                _const_spec((d, 2 * f)),
                _const_spec((f, d))]
    args = [h, mod, g.reshape(1, d), wgu, wd]
    if g_final is not None:
        in_specs.append(_const_spec((1, d)))
        args.append(g_final.reshape(1, d))
    return pl.pallas_call(
        functools.partial(_ffn_kernel, mod_row=mod_row, n_chunks=FFN_CHUNKS),
        grid=(bsz, s // tm),
        in_specs=in_specs,
        out_specs=pl.BlockSpec((1, tm, d), lambda b, j: (b, j, 0)),
        out_shape=jax.ShapeDtypeStruct((bsz, s, d), F32),
        compiler_params=_params(("parallel", "parallel")),
        name=name,
    )(*args)


def _scan_rows(g, forward):
    n_tiles = g.shape[0] // SUBLANE
    rid = lax.broadcasted_iota(jnp.int32, (SUBLANE, LANE), 0)
    tiles = [g[SUBLANE * t:SUBLANE * (t + 1), :] for t in range(n_tiles)]
    for d in [1 << s for s in range(SUBLANE.bit_length() - 1)]:
        if forward:
            tiles = [x + jnp.where(rid >= d, pltpu.roll(x, d, axis=0), 0.0) for x in tiles]
        else:
            tiles = [x + jnp.where(rid < SUBLANE - d, pltpu.roll(x, SUBLANE - d, axis=0), 0.0) for x in tiles]
    order = range(n_tiles) if forward else range(n_tiles - 1, -1, -1)
    edge = SUBLANE - 1 if forward else 0
    carry = None
    for t in order:
        if carry is not None:
            tiles[t] = tiles[t] + carry
        carry = jnp.broadcast_to(tiles[t][edge:edge + 1, :], (SUBLANE, LANE))
    return jnp.concatenate(tiles, axis=0)


def _inproj_kernel(x_ref, mod_ref, g_ref, w_ref, wlr_ref, wal_ref, bal_ref, *out_refs, segs, chunk):
    u = _modulate(x_ref[0], g_ref[...], mod_ref, 3).astype(BF16)
    lr = _dot(u, wlr_ref[...]).astype(BF16)
    z = _dot(lr, wal_ref[...]) + bal_ref[...]
    log_sig = jnp.minimum(z, 0.0) - jnp.log1p(jnp.exp(-jnp.abs(z)))
    g = log_sig * (1.0 / GLA_TAU)
    tm, n = g.shape
    for c0 in range(0, n, LANE):
        for r0 in range(0, tm, chunk):
            out_refs[-1][0, c0 // LANE, r0:r0 + chunk, :] = _scan_rows(g[r0:r0 + chunk, c0:c0 + LANE], c0 < n // 2)
    off = 0
    for (kind, width), o_ref in zip(segs, out_refs[:-1]):
        if kind == "glu":
            a = _dot(u, w_ref[:, off:off + width])
            b = _dot(u, w_ref[:, off + width:off + 2 * width])
            o_ref[0] = (a * jax.nn.sigmoid(b)).astype(o_ref.dtype)
            off += 2 * width
        else:
            p = _dot(u, w_ref[:, off:off + width])
            if kind == "sigmoid":
                p = jax.nn.sigmoid(p)
            if kind == "heads":
                hw = width // GLA_HEADS
                for h in range(GLA_HEADS):
                    o_ref[0, h] = p[:, h * hw:(h + 1) * hw].astype(o_ref.dtype)
            else:
                o_ref[0] = p.astype(o_ref.dtype)
            off += width


def _inproj_call(x, mod, mod_per_batch, g, w, wlr, wal, bal, segs, tm, chunk, name):
    bsz, s, d = x.shape
    assert tm % chunk == 0 and chunk % SUBLANE == 0
    mod_idx = (lambda b, j: (b, 0, 0)) if mod_per_batch else (lambda b, j: (0, 0, 0))
    n_dec = wal.shape[1]
    tok = lambda n: pl.BlockSpec((1, tm, n), lambda b, j: (b, j, 0))
    heads = lambda nh, w: pl.BlockSpec((1, nh, tm, w), lambda b, j: (b, 0, j, 0))
    out_shape, out_specs = [], []
    for kind, width in segs:
        if kind == "heads":
            out_shape.append(jax.ShapeDtypeStruct((bsz, GLA_HEADS, s, width // GLA_HEADS), BF16))
            out_specs.append(heads(GLA_HEADS, width // GLA_HEADS))
        else:
            out_shape.append(jax.ShapeDtypeStruct((bsz, s, width), BF16))
            out_specs.append(tok(width))
    out_shape.append(jax.ShapeDtypeStruct((bsz, n_dec // LANE, s, LANE), F32))
    out_specs.append(heads(n_dec // LANE, LANE))
    return pl.pallas_call(
        functools.partial(_inproj_kernel, segs=segs, chunk=chunk),
        grid=(bsz, s // tm),
        in_specs=[tok(d),
                  pl.BlockSpec((1, N_MOD, d), mod_idx),
                  _const_spec((1, d)),
                  _const_spec(w.shape),
                  _const_spec(wlr.shape),
                  _const_spec(wal.shape),
                  _const_spec((1, n_dec))],
        out_specs=out_specs,
        out_shape=out_shape,
        compiler_params=_params(("parallel", "parallel")),
        name=name,
    )(x, mod, g.reshape(1, d), w, wlr, wal, bal.reshape(1, n_dec))


def _chunk_rows(i, c):
    return pl.ds(i * c, c) if isinstance(i, int) else pl.ds(pl.multiple_of(i * c, c), c)


def _gla_kernel(*refs, n_operands, chunk, scale):
    operands, scratch = refs[:n_operands], refs[n_operands:]
    for h in range(operands[0].shape[0]):
        _gla_head(*[r.at[h] for r in operands], *scratch, chunk=chunk, scale=scale)


def _gla_head(q_ref, k_ref, v_ref, bf_ref, bb_ref, kc_ref, vc_ref, bfc_ref, bbc_ref, o_ref,
              qm_s, km_s, kl_s, att_s, qb_s, u_s, dec_s, st_s, *, chunk, scale):
    c = chunk
    dk = q_ref.shape[1]
    n = q_ref.shape[0] // c
    nc = kc_ref.shape[0] // c
    mid = c // 2
    row = lax.broadcasted_iota(jnp.int32, (c, c), 0)
    col = lax.broadcasted_iota(jnp.int32, (c, c), 1)
    lower = row >= col
    upper = row <= col

    uc, decc = [], []
    for i in range(nc):
        sl = slice(i * c, (i + 1) * c)
        bf, bb = bfc_ref[sl, :], bbc_ref[sl, :]
        blf, blb = bf[c - 1:c, :], bb[0:1, :]
        kcf = kc_ref[sl, :].astype(F32)
        kl = jnp.concatenate([(kcf * jnp.exp(blf - bf)).astype(BF16),
                              (kcf * jnp.exp(blb - bb)).astype(BF16)], axis=1)
        uc.append(_dot_t(vc_ref[sl, :], kl, 0, 0))
        decc.append((jnp.exp(blf), jnp.exp(blb)))
    stf0 = jnp.zeros((uc[0].shape[0], dk), F32)
    stb0 = stf0
    for i in range(nc):
        j = nc - 1 - i
        stf0 = stf0 * decc[i][0] + uc[i][:, :dk]
        stb0 = stb0 * decc[j][1] + uc[j][:, dk:]

    def factors(i):
        sl = _chunk_rows(i, c)
        bf, bb = bf_ref[sl, :], bb_ref[sl, :]
        blf, blb = bf[c - 1:c, :], bb[0:1, :]
        bmf, bmb = bf[mid:mid + 1, :], bb[mid:mid + 1, :]
        q = q_ref[sl, :].astype(F32) * scale
        k = k_ref[sl, :].astype(F32)
        qmf = q * jnp.exp(bf - bmf)
        qmb = q * jnp.exp(bb - bmb)
        kmf = k * jnp.exp(bmf - bf)
        kmb = k * jnp.exp(bmb - bb)
        qm_s[sl, :] = jnp.concatenate([qmf.astype(BF16), qmb.astype(BF16)], axis=1)
        km_s[sl, :] = jnp.concatenate([kmf.astype(BF16), kmb.astype(BF16)], axis=1)
        qb_s[sl, :] = jnp.concatenate([(qmf * jnp.exp(bmf)).astype(BF16),
                                       (qmb * jnp.exp(bmb)).astype(BF16)], axis=1)
        kl_s[sl, :] = jnp.concatenate([(kmf * jnp.exp(blf - bmf)).astype(BF16),
                                       (kmb * jnp.exp(blb - bmb)).astype(BF16)], axis=1)
        dec_s[i] = jnp.concatenate([jnp.exp(blf), jnp.exp(blb)], axis=1)

    def products(i):
        sl = _chunk_rows(i, c)
        qm, km = qm_s[sl, :], km_s[sl, :]
        att = (jnp.where(lower, _dot_t(qm[:, 0:dk], km[:, 0:dk], 1, 1), 0.0)
               + jnp.where(upper, _dot_t(qm[:, dk:2 * dk], km[:, dk:2 * dk], 1, 1), 0.0))
        att_s[sl, :] = att.astype(BF16)
        u_s[i] = _dot_t(v_ref[sl, :], kl_s[sl, :], 0, 0)

    def prep(i, carry):
        products(i - 1)
        factors(i)
        return carry

    factors(0)
    lax.fori_loop(1, n, prep, 0, unroll=3)
    products(n - 1)

    def scan(i, carry):
        stf, stb = carry
        j = n - 1 - i
        st_s[i, :, 0:dk] = stf.astype(BF16)
        st_s[j, :, dk:2 * dk] = stb.astype(BF16)
        stf = stf * dec_s[i][:, 0:dk] + u_s[i, :, 0:dk]
        stb = stb * dec_s[j][:, dk:2 * dk] + u_s[j, :, dk:2 * dk]
        return stf, stb

    lax.fori_loop(0, n, scan, (stf0, stb0))

    def emit(i, carry):
        sl = _chunk_rows(i, c)
        o_ref[sl, :] = _dot(att_s[sl, :], v_ref[sl, :]) + _dot_t(qb_s[sl, :], st_s[i], 1, 1)
        return carry

    lax.fori_loop(0, n, emit, 0, unroll=4)


def _gla_call(q, k, v, gd, kc, vc, gdc, chunk):
    bsz, nh, s, dk = q.shape
    dv = v.shape[3]
    sc = kc.shape[2]
    n = s // chunk
    hps = GLA_HEADS_PER_STEP
    assert s % chunk == 0 and sc % chunk == 0 and dk == LANE and gd.shape[1] == 2 * nh and nh % hps == 0
    hd = lambda rows, w, shift=0: pl.BlockSpec((None, hps, rows, w), lambda b, g: (b, g + shift, 0, 0))
    return pl.pallas_call(
        functools.partial(_gla_kernel, n_operands=10, chunk=chunk, scale=dk ** -0.5),
        grid=(bsz, nh // hps),
        in_specs=[hd(s, dk), hd(s, dk), hd(s, dv), hd(s, dk), hd(s, dk, nh // hps),
                  hd(sc, dk), hd(sc, dv), hd(sc, dk), hd(sc, dk, nh // hps)],
        out_specs=hd(s, dv),
        out_shape=jax.ShapeDtypeStruct((bsz, nh, s, dv), F32),
        scratch_shapes=[pltpu.VMEM((s, 2 * dk), BF16),
                        pltpu.VMEM((s, 2 * dk), BF16),
                        pltpu.VMEM((s, 2 * dk), BF16),
                        pltpu.VMEM((s, chunk), BF16),
                        pltpu.VMEM((s, 2 * dk), BF16),
                        pltpu.VMEM((n, dv, 2 * dk), F32),
                        pltpu.VMEM((n, 1, 2 * dk), F32),
                        pltpu.VMEM((n, dv, 2 * dk), BF16)],
        compiler_params=_params(("parallel", "parallel")),
        name="gla",
    )(q, k, v, gd, gd, kc, vc, gdc, gdc)


def _conv_geometry(width, halo):
    base = halo - width // 2
    n_a = -(-(width + base) // SUBLANE)
    return base, n_a, SUBLANE * (n_a - 1)


def _shift_selector(block, extra):
    span = block + extra
    rows = jnp.arange(SUBLANE * span)
    src = rows % span + rows // span
    return (jnp.arange(span + SUBLANE)[None, :] == src[:, None]).astype(BF16)


def _depthwise_conv(win_ref, sel_ref, sh_ref, dww_ref, out_ref, halo, row_block):
    tm, dc = out_ref.shape
    width = dww_ref.shape[0]
    base, n_a, extra = _conv_geometry(width, halo)
    span = sel_ref.shape[0] // SUBLANE
    block = span - extra
    for bi, r0 in enumerate(range(0, tm, block)):
        sh = sh_ref.at[bi % sh_ref.shape[0]]
        sh[...] = _dot(sel_ref[...], win_ref[r0:r0 + span + SUBLANE, :])
        for c0 in range(0, dc, LANE):
            w = dww_ref[:, c0:c0 + LANE]
            for q0 in range(0, block, row_block):
                acc = None
                for phase in range(SUBLANE):
                    for a in range(n_a):
                        d = SUBLANE * a + phase - base
                        if not 0 <= d < width:
                            continue
                        lo = phase * span + SUBLANE * a + q0
                        term = sh[lo:lo + row_block, c0:c0 + LANE] * w[d:d + 1, :]
                        acc = term if acc is None else acc + term
                out_ref[r0 + q0:r0 + q0 + row_block, c0:c0 + LANE] = acc


def _conv_branch(zc_ref, sel_ref, j, nj, tm, halo, win_ref, sh_ref, conv_ref, dww_ref, dwb_ref, lng_ref, lnb_ref):
    seq = zc_ref.shape[1]
    t0 = pl.multiple_of(j * tm, tm)
    prev_lo = pl.multiple_of(jnp.maximum(t0 - halo, 0), halo)
    next_lo = pl.multiple_of(jnp.minimum(t0 + tm, seq - halo), halo)
    prev = zc_ref[0, pl.ds(prev_lo, halo), :]
    nxt = zc_ref[0, pl.ds(next_lo, halo), :]
    win_ref[0:halo, :] = jnp.where(j > 0, prev, jnp.zeros_like(prev))
    win_ref[halo:halo + tm, :] = zc_ref[0, pl.ds(t0, tm), :]
    win_ref[halo + tm:, :] = jnp.where(j < nj - 1, nxt, jnp.zeros_like(nxt))

    _depthwise_conv(win_ref, sel_ref, sh_ref, dww_ref, conv_ref, halo, 8 * SUBLANE)
    z = conv_ref[...] + dwb_ref[...]
    mu = jnp.mean(z, axis=-1, keepdims=True)
    zc = z - mu
    var = jnp.mean(zc * zc, axis=-1, keepdims=True)
    z = zc * lax.rsqrt(var + EPS) * lng_ref[...] + lnb_ref[...]
    return _silu(z).astype(BF16)


def _merge_kernel(zc_ref, sel_ref, o_ref, og_ref, sga_ref, sgb_ref, x_ref, mod_ref, dww_ref, dwb_ref, lng_ref,
                  lnb_ref, gng_ref, wco_ref, wgo_ref, wo_ref, out_ref, win_ref, sh_ref, conv_ref, *, halo):
    tm = x_ref.shape[1]
    act = _conv_branch(zc_ref, sel_ref, pl.program_id(1), pl.num_programs(1), tm, halo, win_ref, sh_ref, conv_ref,
                       dww_ref, dwb_ref, lng_ref, lnb_ref)
    y_conv = _dot(act, wco_ref[...])

    parts = []
    for h in range(o_ref.shape[1]):
        oh = o_ref[0, h]
        parts.append(oh * lax.rsqrt(jnp.mean(oh * oh, axis=-1, keepdims=True) + EPS))
    on = jnp.concatenate(parts, axis=1) * gng_ref[...]
    y_gla = _dot((on * _silu(og_ref[0].astype(F32))).astype(BF16), wgo_ref[...])

    merged = sga_ref[0].astype(F32) * y_conv + sgb_ref[0].astype(F32) * y_gla
    mix = _dot(merged.astype(BF16), wo_ref[...])
    out_ref[0] = x_ref[0] + mod_ref[0, 5:6, :] * mix


def _merge_call(zc, o, og, sga, sgb, x1, mod, dw_w, dw_b, ln_g, ln_b, gn_g, wco, wgo, wo, tm):
    bsz, s, d = x1.shape
    dc = zc.shape[2]
    halo = 2 * SUBLANE
    block = LANE
    _, _, extra = _conv_geometry(dw_w.shape[0], halo)
    assert dw_w.shape[0] // 2 <= halo and s % tm == 0 and tm % block == 0 and extra + SUBLANE <= 2 * halo
    sel = _shift_selector(block, extra)
    tok = lambda n: pl.BlockSpec((1, tm, n), lambda b, j: (b, j, 0))
    row = lambda a: a.reshape(1, -1)
    return pl.pallas_call(
        functools.partial(_merge_kernel, halo=halo),
        grid=(bsz, s // tm),
        in_specs=[pl.BlockSpec((1, s, dc), lambda b, j: (b, 0, 0)),
                  _const_spec(sel.shape),
                  pl.BlockSpec((1,) + (o.shape[1], tm, o.shape[3]), lambda b, j: (b, 0, j, 0)),
                  tok(og.shape[2]), tok(d), tok(d), tok(d),
                  pl.BlockSpec((1, N_MOD, d), lambda b, j: (b, 0, 0)),
                  _const_spec(dw_w.shape), _const_spec((1, dc)), _const_spec((1, dc)), _const_spec((1, dc)),
                  _const_spec((1, og.shape[2])),
                  _const_spec(wco.shape), _const_spec(wgo.shape), _const_spec(wo.shape)],
        out_specs=tok(d),
        out_shape=jax.ShapeDtypeStruct((bsz, s, d), F32),
        scratch_shapes=[pltpu.VMEM((tm + 2 * halo, dc), zc.dtype),
                        pltpu.VMEM((2,) + (sel.shape[0], dc), F32),
                        pltpu.VMEM((tm, dc), F32)],
        compiler_params=_params(("parallel", "arbitrary")),
        name="merge",
    )(zc, sel, o, og, sga, sgb, x1, mod, dw_w, row(dw_b), row(ln_g), row(ln_b), row(gn_g), wco, wgo, wo)


def kernel(x, c, ctx, c_ctx, w_mod, b_mod, g_ffn1, w1_gu, w1_down, g_mix, w_in, dw_weight, dw_bias,
           conv_ln_g, conv_ln_b, w_conv_out, w_alpha_f, b_alpha_f, w_alpha_b, b_alpha_b, gla_norm_g,
           w_gla_out, w_out, g_ffn2, w2_gu, w2_down, g_final):
    depth = w_mod.shape[0]
    assert depth == 1, "context-side mixing of non-final layers is not implemented"
    bsz, seq, d = x.shape
    d_conv = dw_weight.shape[2]
    qk_w = w_alpha_f.shape[2]
    v_w = gla_norm_g.shape[1]
    r = GLA_LOWRANK

    pad_rows = (-(bsz + 1)) % SUBLANE
    c_all = jnp.concatenate([c, c_ctx[None, :], jnp.zeros((pad_rows, d), F32)], axis=0)
    mod = _mod_call(c_all, w_mod[0], b_mod[0])
    mod_x = mod[:bsz].reshape(bsz, N_MOD, d)
    mod_c = mod[bsz:bsz + 1].reshape(1, N_MOD, d)

    w1gu, w1d = w1_gu[0].astype(BF16), w1_down[0].astype(BF16)
    w2gu, w2d = w2_gu[0].astype(BF16), w2_down[0].astype(BF16)
    wco, wgo, wo = w_conv_out[0].astype(BF16), w_gla_out[0].astype(BF16), w_out[0].astype(BF16)
    wi = w_in[0]
    o_q = 2 * d_conv
    o_k, o_v = o_q + qk_w, o_q + 2 * qk_w
    o_og = o_v + v_w
    o_af = o_og + v_w
    o_ga = o_af + 2 * r
    w_x = jnp.concatenate([wi[:, :o_af], wi[:, o_ga:]], axis=1).astype(BF16)
    w_c = wi[:, o_k:o_og].astype(BF16)
    wlr = jnp.concatenate([wi[:, o_af:o_ga], jnp.zeros((d, LANE - 2 * r), F32)], axis=1).astype(BF16)
    wal = jnp.zeros((LANE, 2 * qk_w), F32)
    wal = wal.at[:r, :qk_w].set(w_alpha_f[0]).at[r:2 * r, qk_w:].set(w_alpha_b[0]).astype(BF16)
    bal = jnp.concatenate([b_alpha_f[0], b_alpha_b[0]])

    x1 = _ffn_call(x, mod_x, True, g_ffn1[0], w1gu, w1d, 0, 1024, name="ffn1_x")
    h1 = _ffn_call(ctx, mod_c, False, g_ffn1[0], w1gu, w1d, 0, ctx.shape[1], name="ffn1_ctx")

    segs_x = (("glu", d_conv), ("heads", qk_w), ("heads", qk_w), ("heads", v_w), ("plain", v_w),
              ("sigmoid", d), ("sigmoid", d))
    zc, q, k, v, og, sga, sgb, gd = _inproj_call(x1, mod_x, True, g_mix[0], w_x, wlr, wal, bal,
                                                 segs_x, 512, GLA_CHUNK, "inproj_x")
    segs_c = (("heads", qk_w), ("heads", v_w))
    kc, vc, gdc = _inproj_call(h1, mod_c, False, g_mix[0], w_c, wlr, wal, bal, segs_c,
                               ctx.shape[1], GLA_CHUNK, "inproj_ctx")

    o = _gla_call(q, k, v, gd, kc, vc, gdc, GLA_CHUNK)

    x2 = _merge_call(zc, o, og, sga, sgb, x1, mod_x, dw_weight[0], dw_bias[0], conv_ln_g[0],
                     conv_ln_b[0], gla_norm_g[0], wco, wgo, wo, 256)

    return _ffn_call(x2, mod_x, True, g_ffn2[0], w2gu, w2d, 6, 1024, "ffn2_x", g_final=g_final)
```

```python
import functools

import jax
import jax.numpy as jnp
from jax import lax
from jax.experimental import pallas as pl
from jax.experimental.pallas import tpu as pltpu

F32 = jnp.float32
BF16 = jnp.bfloat16

EPS = 1e-6
N_MOD = 9
GLA_HEADS = 4
GLA_TAU = 16.0
GLA_LOWRANK = 16
GLA_CHUNK = 128
GLA_HEADS_PER_STEP = 2
LANE = 128
SUBLANE = 8
MXU_K = 256
FFN_CHUNKS = 2
VMEM_LIMIT = 56 * 1024 * 1024


def _dot(a, b):
    return jnp.dot(a, b, preferred_element_type=F32)


def _dot_t(a, b, ca, cb):
    return lax.dot_general(a, b, (((ca,), (cb,)), ((), ())), preferred_element_type=F32)


def _silu(x):
    return x * jax.nn.sigmoid(x)


def _rms(h, g):
    return h * lax.rsqrt(jnp.mean(h * h, axis=-1, keepdims=True) + EPS) * g


def _modulate(h, g, mod_ref, row):
    shift = mod_ref[0, row:row + 1, :]
    scale = mod_ref[0, row + 1:row + 2, :]
    return _rms(h, g) * (1.0 + scale) + shift


def _const_spec(shape):
    nd = len(shape)
    return pl.BlockSpec(shape, lambda *_: (0,) * nd, pipeline_mode=pl.Buffered(1))


def _params(sem, flags=None):
    return pltpu.CompilerParams(dimension_semantics=sem, vmem_limit_bytes=VMEM_LIMIT, flags=flags)


def _mod_kernel(c_ref, w_ref, b_ref, o_ref):
    s = _silu(c_ref[...]).astype(BF16)
    o_ref[...] = _dot(s, w_ref[...].astype(BF16)) + b_ref[...]


def _mod_call(c_all, w_mod, b_mod, n_steps=8):
    rows, d = c_all.shape
    n = w_mod.shape[1]
    bn = n // n_steps
    return pl.pallas_call(
        _mod_kernel,
        grid=(n_steps,),
        in_specs=[pl.BlockSpec((rows, d), lambda j: (0, 0)),
                  pl.BlockSpec((d, bn), lambda j: (0, j)),
                  pl.BlockSpec((1, bn), lambda j: (0, j))],
        out_specs=pl.BlockSpec((rows, bn), lambda j: (0, j)),
        out_shape=jax.ShapeDtypeStruct((rows, n), F32),
        compiler_params=_params(("arbitrary",)),
        name="mod",
    )(c_all, w_mod, b_mod.reshape(1, n))


def _ffn_bounds(f, n_chunks):
    tiles = -(-f // MXU_K)
    edges = [min(f, MXU_K * (-(-tiles * j // n_chunks))) for j in range(n_chunks + 1)]
    return list(zip(edges[:-1], edges[1:]))


def _half_ffn(h, mod_ref, g_ref, wgu_ref, wd_ref, mod_row, n_chunks):
    u = _modulate(h, g_ref[...], mod_ref, mod_row).astype(BF16)
    f = wd_ref.shape[0]
    acc = None
    for lo, hi in _ffn_bounds(f, n_chunks):
        a = _dot(u, wgu_ref[:, lo:hi])
        b = _dot(u, wgu_ref[:, f + lo:f + hi])
        m = (_silu(a) * b).astype(BF16)
        d = _dot(m, wd_ref[lo:hi, :])
        acc = d if acc is None else acc + d
    gate = mod_ref[0, mod_row + 2:mod_row + 3, :]
    return h + (0.5 * gate) * acc


def _ffn_kernel(h_ref, mod_ref, g_ref, wgu_ref, wd_ref, *rest, mod_row, n_chunks):
    gfin_ref, o_ref = rest if len(rest) == 2 else (None, rest[0])
    y = _half_ffn(h_ref[0], mod_ref, g_ref, wgu_ref, wd_ref, mod_row, n_chunks)
    o_ref[0] = y if gfin_ref is None else _rms(y, gfin_ref[...])


def _ffn_call(h, mod, mod_per_batch, g, wgu, wd, mod_row, tm, name, g_final=None):
    bsz, s, d = h.shape
    f = wd.shape[0]
    mod_idx = (lambda b, j: (b, 0, 0)) if mod_per_batch else (lambda b, j: (0, 0, 0))
    in_specs = [pl.BlockSpec((1, tm, d), lambda b, j: (b, j, 0)),
                pl.BlockSpec((1, N_MOD, d), mod_idx),
                _const_spec((1, d)),
                _const_spec((d, 2 * f)),
                _const_spec((f, d))]
    args = [h, mod, g.reshape(1, d), wgu, wd]
    if g_final is not None:
        in_specs.append(_const_spec((1, d)))
        args.append(g_final.reshape(1, d))
    return pl.pallas_call(
        functools.partial(_ffn_kernel, mod_row=mod_row, n_chunks=FFN_CHUNKS),
        grid=(bsz, s // tm),
        in_specs=in_specs,
        out_specs=pl.BlockSpec((1, tm, d), lambda b, j: (b, j, 0)),
        out_shape=jax.ShapeDtypeStruct((bsz, s, d), F32),
        compiler_params=_params(("parallel", "parallel")),
        name=name,
    )(*args)


def _scan_rows(g, forward):
    n_tiles = g.shape[0] // SUBLANE
    rid = lax.broadcasted_iota(jnp.int32, (SUBLANE, LANE), 0)
    tiles = [g[SUBLANE * t:SUBLANE * (t + 1), :] for t in range(n_tiles)]
    for d in [1 << s for s in range(SUBLANE.bit_length() - 1)]:
        if forward:
            tiles = [x + jnp.where(rid >= d, pltpu.roll(x, d, axis=0), 0.0) for x in tiles]
        else:
            tiles = [x + jnp.where(rid < SUBLANE - d, pltpu.roll(x, SUBLANE - d, axis=0), 0.0) for x in tiles]
    order = range(n_tiles) if forward else range(n_tiles - 1, -1, -1)
    edge = SUBLANE - 1 if forward else 0
    carry = None
    for t in order:
        if carry is not None:
            tiles[t] = tiles[t] + carry
        carry = jnp.broadcast_to(tiles[t][edge:edge + 1, :], (SUBLANE, LANE))
    return jnp.concatenate(tiles, axis=0)


def _inproj_kernel(x_ref, mod_ref, g_ref, w_ref, wlr_ref, wal_ref, bal_ref, *out_refs, segs, chunk):
    u = _modulate(x_ref[0], g_ref[...], mod_ref, 3).astype(BF16)
    lr = _dot(u, wlr_ref[...]).astype(BF16)
    z = _dot(lr, wal_ref[...]) + bal_ref[...]
    log_sig = jnp.minimum(z, 0.0) - jnp.log1p(jnp.exp(-jnp.abs(z)))
    g = log_sig * (1.0 / GLA_TAU)
    tm, n = g.shape
    for c0 in range(0, n, LANE):
        for r0 in range(0, tm, chunk):
            out_refs[-1][0, c0 // LANE, r0:r0 + chunk, :] = _scan_rows(g[r0:r0 + chunk, c0:c0 + LANE], c0 < n // 2)
    off = 0
    for (kind, width), o_ref in zip(segs, out_refs[:-1]):
        if kind == "glu":
            a = _dot(u, w_ref[:, off:off + width])
            b = _dot(u, w_ref[:, off + width:off + 2 * width])
            o_ref[0] = (a * jax.nn.sigmoid(b)).astype(o_ref.dtype)
            off += 2 * width
        else:
            p = _dot(u, w_ref[:, off:off + width])
            if kind == "sigmoid":
                p = jax.nn.sigmoid(p)
            if kind == "heads":
                hw = width // GLA_HEADS
                for h in range(GLA_HEADS):
                    o_ref[0, h] = p[:, h * hw:(h + 1) * hw].astype(o_ref.dtype)
            else:
                o_ref[0] = p.astype(o_ref.dtype)
            off += width


def _inproj_call(x, mod, mod_per_batch, g, w, wlr, wal, bal, segs, tm, chunk, name):
    bsz, s, d = x.shape
    assert tm % chunk == 0 and chunk % SUBLANE == 0
    mod_idx = (lambda b, j: (b, 0, 0)) if mod_per_batch else (lambda b, j: (0, 0, 0))
    n_dec = wal.shape[1]
    tok = lambda n: pl.BlockSpec((1, tm, n), lambda b, j: (b, j, 0))
    heads = lambda nh, w: pl.BlockSpec((1, nh, tm, w), lambda b, j: (b, 0, j, 0))
    out_shape, out_specs = [], []
    for kind, width in segs:
        if kind == "heads":
            out_shape.append(jax.ShapeDtypeStruct((bsz, GLA_HEADS, s, width // GLA_HEADS), BF16))
            out_specs.append(heads(GLA_HEADS, width // GLA_HEADS))
        else:
            out_shape.append(jax.ShapeDtypeStruct((bsz, s, width), BF16))
            out_specs.append(tok(width))
    out_shape.append(jax.ShapeDtypeStruct((bsz, n_dec // LANE, s, LANE), F32))
    out_specs.append(heads(n_dec // LANE, LANE))
    return pl.pallas_call(
        functools.partial(_inproj_kernel, segs=segs, chunk=chunk),
        grid=(bsz, s // tm),
        in_specs=[tok(d),
                  pl.BlockSpec((1, N_MOD, d), mod_idx),
                  _const_spec((1, d)),
                  _const_spec(w.shape),
                  _const_spec(wlr.shape),
                  _const_spec(wal.shape),
                  _const_spec((1, n_dec))],
        out_specs=out_specs,
        out_shape=out_shape,
        compiler_params=_params(("parallel", "parallel")),
        name=name,
    )(x, mod, g.reshape(1, d), w, wlr, wal, bal.reshape(1, n_dec))


def _chunk_rows(i, c):
    return pl.ds(i * c, c) if isinstance(i, int) else pl.ds(pl.multiple_of(i * c, c), c)


def _gla_kernel(*refs, n_operands, chunk, scale):
    operands, scratch = refs[:n_operands], refs[n_operands:]
    for h in range(operands[0].shape[0]):
        _gla_head(*[r.at[h] for r in operands], *[r.at[h] for r in scratch], chunk=chunk, scale=scale)


def _gla_head(q_ref, k_ref, v_ref, bf_ref, bb_ref, kc_ref, vc_ref, bfc_ref, bbc_ref, o_ref,
              qm_s, km_s, kl_s, att_s, qb_s, u_s, dec_s, st_s, *, chunk, scale):
    c = chunk
    dk = q_ref.shape[1]
    n = q_ref.shape[0] // c
    nc = kc_ref.shape[0] // c
    mid = c // 2
    row = lax.broadcasted_iota(jnp.int32, (c, c), 0)
    col = lax.broadcasted_iota(jnp.int32, (c, c), 1)
    lower = row >= col
    upper = row <= col

    uc, decc = [], []
    for i in range(nc):
        sl = slice(i * c, (i + 1) * c)
        bf, bb = bfc_ref[sl, :], bbc_ref[sl, :]
        blf, blb = bf[c - 1:c, :], bb[0:1, :]
        kcf = kc_ref[sl, :].astype(F32)
        kl = jnp.concatenate([(kcf * jnp.exp(blf - bf)).astype(BF16),
                              (kcf * jnp.exp(blb - bb)).astype(BF16)], axis=1)
        uc.append(_dot_t(vc_ref[sl, :], kl, 0, 0))
        decc.append((jnp.exp(blf), jnp.exp(blb)))
    stf0 = jnp.zeros((uc[0].shape[0], dk), F32)
    stb0 = stf0
    for i in range(nc):
        j = nc - 1 - i
        stf0 = stf0 * decc[i][0] + uc[i][:, :dk]
        stb0 = stb0 * decc[j][1] + uc[j][:, dk:]

    def factors(i):
        sl = _chunk_rows(i, c)
        bf, bb = bf_ref[sl, :], bb_ref[sl, :]
        blf, blb = bf[c - 1:c, :], bb[0:1, :]
        bmf, bmb = bf[mid:mid + 1, :], bb[mid:mid + 1, :]
        q = q_ref[sl, :].astype(F32) * scale
        k = k_ref[sl, :].astype(F32)
        qmf = q * jnp.exp(bf - bmf)
        qmb = q * jnp.exp(bb - bmb)
        kmf = k * jnp.exp(bmf - bf)
        kmb = k * jnp.exp(bmb - bb)
        qm_s[sl, :] = jnp.concatenate([qmf.astype(BF16), qmb.astype(BF16)], axis=1)
        km_s[sl, :] = jnp.concatenate([kmf.astype(BF16), kmb.astype(BF16)], axis=1)
        qb_s[sl, :] = jnp.concatenate([(qmf * jnp.exp(bmf)).astype(BF16),
                                       (qmb * jnp.exp(bmb)).astype(BF16)], axis=1)
        kl_s[sl, :] = jnp.concatenate([(kmf * jnp.exp(blf - bmf)).astype(BF16),
                                       (kmb * jnp.exp(blb - bmb)).astype(BF16)], axis=1)
        dec_s[i] = jnp.concatenate([jnp.exp(blf), jnp.exp(blb)], axis=1)

    def products(i):
        sl = _chunk_rows(i, c)
        qm, km = qm_s[sl, :], km_s[sl, :]
        att = (jnp.where(lower, _dot_t(qm[:, 0:dk], km[:, 0:dk], 1, 1), 0.0)
               + jnp.where(upper, _dot_t(qm[:, dk:2 * dk], km[:, dk:2 * dk], 1, 1), 0.0))
        att_s[sl, :] = att.astype(BF16)
        u_s[i] = _dot_t(v_ref[sl, :], kl_s[sl, :], 0, 0)

    def prep(i, carry):
        products(i - 1)
        factors(i)
        return carry

    factors(0)
    lax.fori_loop(1, n, prep, 0, unroll=3)
    products(n - 1)

    def scan(i, carry):
        stf, stb = carry
        j = n - 1 - i
        st_s[i, :, 0:dk] = stf.astype(BF16)
        st_s[j, :, dk:2 * dk] = stb.astype(BF16)
        stf = stf * dec_s[i][:, 0:dk] + u_s[i, :, 0:dk]
        stb = stb * dec_s[j][:, dk:2 * dk] + u_s[j, :, dk:2 * dk]
        return stf, stb

    lax.fori_loop(0, n, scan, (stf0, stb0))

    def emit(i, carry):
        sl = _chunk_rows(i, c)
        o_ref[sl, :] = _dot(att_s[sl, :], v_ref[sl, :]) + _dot_t(qb_s[sl, :], st_s[i], 1, 1)
        return carry

    lax.fori_loop(0, n, emit, 0, unroll=16)


def _gla_call(q, k, v, gd, kc, vc, gdc, chunk):
    bsz, nh, s, dk = q.shape
    dv = v.shape[3]
    sc = kc.shape[2]
    n = s // chunk
    hps = GLA_HEADS_PER_STEP
    assert s % chunk == 0 and sc % chunk == 0 and dk == LANE and gd.shape[1] == 2 * nh and nh % hps == 0
    hd = lambda rows, w, shift=0: pl.BlockSpec((None, hps, rows, w), lambda b, g: (b, g + shift, 0, 0))
    return pl.pallas_call(
        functools.partial(_gla_kernel, n_operands=10, chunk=chunk, scale=dk ** -0.5),
        grid=(bsz, nh // hps),
        in_specs=[hd(s, dk), hd(s, dk), hd(s, dv), hd(s, dk), hd(s, dk, nh // hps),
                  hd(sc, dk), hd(sc, dv), hd(sc, dk), hd(sc, dk, nh // hps)],
        out_specs=hd(s, dv),
        out_shape=jax.ShapeDtypeStruct((bsz, nh, s, dv), F32),
        scratch_shapes=[pltpu.VMEM((hps, s, 2 * dk), BF16),
                        pltpu.VMEM((hps, s, 2 * dk), BF16),
                        pltpu.VMEM((hps, s, 2 * dk), BF16),
                        pltpu.VMEM((hps, s, chunk), BF16),
                        pltpu.VMEM((hps, s, 2 * dk), BF16),
                        pltpu.VMEM((hps, n, dv, 2 * dk), F32),
                        pltpu.VMEM((hps, n, 1, 2 * dk), F32),
                        pltpu.VMEM((hps, n, dv, 2 * dk), BF16)],
        compiler_params=_params(("parallel", "parallel")),
        name="gla",
    )(q, k, v, gd, gd, kc, vc, gdc, gdc)


def _conv_geometry(width, halo):
    base = halo - width // 2
    n_a = -(-(width + base) // SUBLANE)
    return base, n_a, SUBLANE * (n_a - 1)


def _shift_selector(block, extra):
    span = block + extra
    rows = jnp.arange(SUBLANE * span)
    src = rows % span + rows // span
    return (jnp.arange(span + SUBLANE)[None, :] == src[:, None]).astype(BF16)


def _depthwise_conv(win_ref, sel_ref, sh_ref, dww_ref, out_ref, halo, row_block):
    tm, dc = out_ref.shape
    width = dww_ref.shape[0]
    base, n_a, extra = _conv_geometry(width, halo)
    span = sel_ref.shape[0] // SUBLANE
    block = span - extra
    for bi, r0 in enumerate(range(0, tm, block)):
        sh = sh_ref.at[bi % sh_ref.shape[0]]
        sh[...] = _dot(sel_ref[...], win_ref[r0:r0 + span + SUBLANE, :])
        for c0 in range(0, dc, LANE):
            w = dww_ref[:, c0:c0 + LANE]
            for q0 in range(0, block, row_block):
                acc = None
                for phase in range(SUBLANE):
                    for a in range(n_a):
                        d = SUBLANE * a + phase - base
                        if not 0 <= d < width:
                            continue
                        lo = phase * span + SUBLANE * a + q0
                        term = sh[lo:lo + row_block, c0:c0 + LANE] * w[d:d + 1, :]
                        acc = term if acc is None else acc + term
                out_ref[r0 + q0:r0 + q0 + row_block, c0:c0 + LANE] = acc


def _conv_branch(zc_ref, sel_ref, j, nj, tm, halo, win_ref, sh_ref, conv_ref, dww_ref, dwb_ref, lng_ref, lnb_ref):
    seq = zc_ref.shape[1]
    t0 = pl.multiple_of(j * tm, tm)
    prev_lo = pl.multiple_of(jnp.maximum(t0 - halo, 0), halo)
    next_lo = pl.multiple_of(jnp.minimum(t0 + tm, seq - halo), halo)
    prev = zc_ref[0, pl.ds(prev_lo, halo), :]
    nxt = zc_ref[0, pl.ds(next_lo, halo), :]
    win_ref[0:halo, :] = jnp.where(j > 0, prev, jnp.zeros_like(prev))
    win_ref[halo:halo + tm, :] = zc_ref[0, pl.ds(t0, tm), :]
    win_ref[halo + tm:, :] = jnp.where(j < nj - 1, nxt, jnp.zeros_like(nxt))

    _depthwise_conv(win_ref, sel_ref, sh_ref, dww_ref, conv_ref, halo, 8 * SUBLANE)
    z = conv_ref[...] + dwb_ref[...]
    mu = jnp.mean(z, axis=-1, keepdims=True)
    zc = z - mu
    var = jnp.mean(zc * zc, axis=-1, keepdims=True)
    z = zc * lax.rsqrt(var + EPS) * lng_ref[...] + lnb_ref[...]
    return _silu(z).astype(BF16)


def _merge_kernel(zc_ref, sel_ref, o_ref, og_ref, sga_ref, sgb_ref, x_ref, mod_ref, dww_ref, dwb_ref, lng_ref,
                  lnb_ref, gng_ref, wco_ref, wgo_ref, wo_ref, out_ref, win_ref, sh_ref, conv_ref, *, halo):
    tm = x_ref.shape[1]
    act = _conv_branch(zc_ref, sel_ref, pl.program_id(1), pl.num_programs(1), tm, halo, win_ref, sh_ref, conv_ref,
                       dww_ref, dwb_ref, lng_ref, lnb_ref)
    y_conv = _dot(act, wco_ref[...])

    parts = []
    for h in range(o_ref.shape[1]):
        oh = o_ref[0, h]
        parts.append(oh * lax.rsqrt(jnp.mean(oh * oh, axis=-1, keepdims=True) + EPS))
    on = jnp.concatenate(parts, axis=1) * gng_ref[...]
    y_gla = _dot((on * _silu(og_ref[0].astype(F32))).astype(BF16), wgo_ref[...])

    merged = sga_ref[0].astype(F32) * y_conv + sgb_ref[0].astype(F32) * y_gla
    mix = _dot(merged.astype(BF16), wo_ref[...])
    out_ref[0] = x_ref[0] + mod_ref[0, 5:6, :] * mix


def _merge_call(zc, o, og, sga, sgb, x1, mod, dw_w, dw_b, ln_g, ln_b, gn_g, wco, wgo, wo, tm):
    bsz, s, d = x1.shape
    dc = zc.shape[2]
    halo = 2 * SUBLANE
    block = LANE
    _, _, extra = _conv_geometry(dw_w.shape[0], halo)
    assert dw_w.shape[0] // 2 <= halo and s % tm == 0 and tm % block == 0 and extra + SUBLANE <= 2 * halo
    sel = _shift_selector(block, extra)
    tok = lambda n: pl.BlockSpec((1, tm, n), lambda b, j: (b, j, 0))
    row = lambda a: a.reshape(1, -1)
    return pl.pallas_call(
        functools.partial(_merge_kernel, halo=halo),
        grid=(bsz, s // tm),
        in_specs=[pl.BlockSpec((1, s, dc), lambda b, j: (b, 0, 0)),
                  _const_spec(sel.shape),
                  pl.BlockSpec((1,) + (o.shape[1], tm, o.shape[3]), lambda b, j: (b, 0, j, 0)),
                  tok(og.shape[2]), tok(d), tok(d), tok(d),
                  pl.BlockSpec((1, N_MOD, d), lambda b, j: (b, 0, 0)),
                  _const_spec(dw_w.shape), _const_spec((1, dc)), _const_spec((1, dc)), _const_spec((1, dc)),
                  _const_spec((1, og.shape[2])),
                  _const_spec(wco.shape), _const_spec(wgo.shape), _const_spec(wo.shape)],
        out_specs=tok(d),
        out_shape=jax.ShapeDtypeStruct((bsz, s, d), F32),
        scratch_shapes=[pltpu.VMEM((tm + 2 * halo, dc), zc.dtype),
                        pltpu.VMEM((2,) + (sel.shape[0], dc), F32),
                        pltpu.VMEM((tm, dc), F32)],
        compiler_params=_params(("parallel", "arbitrary")),
        name="merge",
    )(zc, sel, o, og, sga, sgb, x1, mod, dw_w, row(dw_b), row(ln_g), row(ln_b), row(gn_g), wco, wgo, wo)


def kernel(x, c, ctx, c_ctx, w_mod, b_mod, g_ffn1, w1_gu, w1_down, g_mix, w_in, dw_weight, dw_bias,
           conv_ln_g, conv_ln_b, w_conv_out, w_alpha_f, b_alpha_f, w_alpha_b, b_alpha_b, gla_norm_g,
           w_gla_out, w_out, g_ffn2, w2_gu, w2_down, g_final):
    depth = w_mod.shape[0]
    assert depth == 1, "context-side mixing of non-final layers is not implemented"
    bsz, seq, d = x.shape
    d_conv = dw_weight.shape[2]
    qk_w = w_alpha_f.shape[2]
    v_w = gla_norm_g.shape[1]
    r = GLA_LOWRANK

    pad_rows = (-(bsz + 1)) % SUBLANE
    c_all = jnp.concatenate([c, c_ctx[None, :], jnp.zeros((pad_rows, d), F32)], axis=0)
    mod = _mod_call(c_all, w_mod[0], b_mod[0])
    mod_x = mod[:bsz].reshape(bsz, N_MOD, d)
    mod_c = mod[bsz:bsz + 1].reshape(1, N_MOD, d)

    w1gu, w1d = w1_gu[0].astype(BF16), w1_down[0].astype(BF16)
    w2gu, w2d = w2_gu[0].astype(BF16), w2_down[0].astype(BF16)
    wco, wgo, wo = w_conv_out[0].astype(BF16), w_gla_out[0].astype(BF16), w_out[0].astype(BF16)
    wi = w_in[0]
    o_q = 2 * d_conv
    o_k, o_v = o_q + qk_w, o_q + 2 * qk_w
    o_og = o_v + v_w
    o_af = o_og + v_w
    o_ga = o_af + 2 * r
    w_x = jnp.concatenate([wi[:, :o_af], wi[:, o_ga:]], axis=1).astype(BF16)
    w_c = wi[:, o_k:o_og].astype(BF16)
    wlr = jnp.concatenate([wi[:, o_af:o_ga], jnp.zeros((d, LANE - 2 * r), F32)], axis=1).astype(BF16)
    wal = jnp.zeros((LANE, 2 * qk_w), F32)
    wal = wal.at[:r, :qk_w].set(w_alpha_f[0]).at[r:2 * r, qk_w:].set(w_alpha_b[0]).astype(BF16)
    bal = jnp.concatenate([b_alpha_f[0], b_alpha_b[0]])

    x1 = _ffn_call(x, mod_x, True, g_ffn1[0], w1gu, w1d, 0, 1024, name="ffn1_x")
    h1 = _ffn_call(ctx, mod_c, False, g_ffn1[0], w1gu, w1d, 0, ctx.shape[1], name="ffn1_ctx")

    segs_x = (("glu", d_conv), ("heads", qk_w), ("heads", qk_w), ("heads", v_w), ("plain", v_w),
              ("sigmoid", d), ("sigmoid", d))
    zc, q, k, v, og, sga, sgb, gd = _inproj_call(x1, mod_x, True, g_mix[0], w_x, wlr, wal, bal,
                                                 segs_x, 512, GLA_CHUNK, "inproj_x")
    segs_c = (("heads", qk_w), ("heads", v_w))
    kc, vc, gdc = _inproj_call(h1, mod_c, False, g_mix[0], w_c, wlr, wal, bal, segs_c,
                               ctx.shape[1], GLA_CHUNK, "inproj_ctx")

    o = _gla_call(q, k, v, gd, kc, vc, gdc, GLA_CHUNK)

    x2 = _merge_call(zc, o, og, sga, sgb, x1, mod_x, dw_weight[0], dw_bias[0], conv_ln_g[0],
                     conv_ln_b[0], gla_norm_g[0], wco, wgo, wo, 256)

    return _ffn_call(x2, mod_x, True, g_ffn2[0], w2gu, w2d, 6, 1024, "ffn2_x", g_final=g_final)
```

```python
import functools

import jax
import jax.numpy as jnp
from jax import lax
from jax.experimental import pallas as pl
from jax.experimental.pallas import tpu as pltpu

F32 = jnp.float32
BF16 = jnp.bfloat16

EPS = 1e-6
N_MOD = 9
GLA_HEADS = 4
GLA_TAU = 16.0
GLA_LOWRANK = 16
GLA_CHUNK = 128
GLA_HEADS_PER_STEP = 2
LANE = 128
SUBLANE = 8
MXU_K = 256
FFN_CHUNKS = 2
VMEM_LIMIT = 56 * 1024 * 1024


def _dot(a, b):
    return jnp.dot(a, b, preferred_element_type=F32)


def _dot_t(a, b, ca, cb):
    return lax.dot_general(a, b, (((ca,), (cb,)), ((), ())), preferred_element_type=F32)


def _silu(x):
    return x * jax.nn.sigmoid(x)


def _rms(h, g):
    return h * lax.rsqrt(jnp.mean(h * h, axis=-1, keepdims=True) + EPS) * g


def _modulate(h, g, mod_ref, row):
    shift = mod_ref[0, row:row + 1, :]
    scale = mod_ref[0, row + 1:row + 2, :]
    return _rms(h, g) * (1.0 + scale) + shift


def _const_spec(shape):
    nd = len(shape)
    return pl.BlockSpec(shape, lambda *_: (0,) * nd, pipeline_mode=pl.Buffered(1))


def _params(sem, flags=None):
    return pltpu.CompilerParams(dimension_semantics=sem, vmem_limit_bytes=VMEM_LIMIT, flags=flags)


def _mod_kernel(c_ref, w_ref, b_ref, o_ref):
    s = _silu(c_ref[...]).astype(BF16)
    o_ref[...] = _dot(s, w_ref[...].astype(BF16)) + b_ref[...]


def _mod_call(c_all, w_mod, b_mod, n_steps=8):
    rows, d = c_all.shape
    n = w_mod.shape[1]
    bn = n // n_steps
    return pl.pallas_call(
        _mod_kernel,
        grid=(n_steps,),
        in_specs=[pl.BlockSpec((rows, d), lambda j: (0, 0)),
                  pl.BlockSpec((d, bn), lambda j: (0, j)),
                  pl.BlockSpec((1, bn), lambda j: (0, j))],
        out_specs=pl.BlockSpec((rows, bn), lambda j: (0, j)),
        out_shape=jax.ShapeDtypeStruct((rows, n), F32),
        compiler_params=_params(("arbitrary",)),
        name="mod",
    )(c_all, w_mod, b_mod.reshape(1, n))


def _ffn_bounds(f, n_chunks):
    tiles = -(-f // MXU_K)
    edges = [min(f, MXU_K * (-(-tiles * j // n_chunks))) for j in range(n_chunks + 1)]
    return list(zip(edges[:-1], edges[1:]))


def _half_ffn(h, mod_ref, g_ref, wgu_ref, wd_ref, mod_row, n_chunks):
    u = _modulate(h, g_ref[...], mod_ref, mod_row).astype(BF16)
    f = wd_ref.shape[0]
    acc = None
    for lo, hi in _ffn_bounds(f, n_chunks):
        a = _dot(u, wgu_ref[:, lo:hi])
        b = _dot(u, wgu_ref[:, f + lo:f + hi])
        m = (_silu(a) * b).astype(BF16)
        d = _dot(m, wd_ref[lo:hi, :])
        acc = d if acc is None else acc + d
    gate = mod_ref[0, mod_row + 2:mod_row + 3, :]
    return h + (0.5 * gate) * acc


def _ffn_kernel(h_ref, mod_ref, g_ref, wgu_ref, wd_ref, *rest, mod_row, n_chunks):
    gfin_ref, o_ref = rest if len(rest) == 2 else (None, rest[0])
    y = _half_ffn(h_ref[0], mod_ref, g_ref, wgu_ref, wd_ref, mod_row, n_chunks)
    o_ref[0] = y if gfin_ref is None else _rms(y, gfin_ref[...])


def _ffn_call(h, mod, mod_per_batch, g, wgu, wd, mod_row, tm, name, g_final=None):
    bsz, s, d = h.shape
    f = wd.shape[0]
    mod_idx = (lambda b, j: (b, 0, 0)) if mod_per_batch else (lambda b, j: (0, 0, 0))
    in_specs = [pl.BlockSpec((1, tm, d), lambda b, j: (b, j, 0)),
                pl.BlockSpec((1, N_MOD, d), mod_idx),
                _const_spec((1, d)),
                _const_spec((d, 2 * f)),
                _const_spec((f, d))]
    args = [h, mod, g.reshape(1, d), wgu, wd]
    if g_final is not None:
        in_specs.append(_const_spec((1, d)))
        args.append(g_final.reshape(1, d))
    return pl.pallas_call(
        functools.partial(_ffn_kernel, mod_row=mod_row, n_chunks=FFN_CHUNKS),
        grid=(bsz, s // tm),
        in_specs=in_specs,
        out_specs=pl.BlockSpec((1, tm, d), lambda b, j: (b, j, 0)),
        out_shape=jax.ShapeDtypeStruct((bsz, s, d), F32),
        compiler_params=_params(("parallel", "parallel")),
        name=name,
    )(*args)


def _scan_rows(g, forward):
    n_tiles = g.shape[0] // SUBLANE
    rid = lax.broadcasted_iota(jnp.int32, (SUBLANE, LANE), 0)
    tiles = [g[SUBLANE * t:SUBLANE * (t + 1), :] for t in range(n_tiles)]
    for d in [1 << s for s in range(SUBLANE.bit_length() - 1)]:
        if forward:
            tiles = [x + jnp.where(rid >= d, pltpu.roll(x, d, axis=0), 0.0) for x in tiles]
        else:
            tiles = [x + jnp.where(rid < SUBLANE - d, pltpu.roll(x, SUBLANE - d, axis=0), 0.0) for x in tiles]
    order = range(n_tiles) if forward else range(n_tiles - 1, -1, -1)
    edge = SUBLANE - 1 if forward else 0
    carry = None
    for t in order:
        if carry is not None:
            tiles[t] = tiles[t] + carry
        carry = jnp.broadcast_to(tiles[t][edge:edge + 1, :], (SUBLANE, LANE))
    return jnp.concatenate(tiles, axis=0)


def _inproj_kernel(x_ref, mod_ref, g_ref, w_ref, wlr_ref, wal_ref, bal_ref, *out_refs, segs, chunk):
    u = _modulate(x_ref[0], g_ref[...], mod_ref, 3).astype(BF16)
    lr = _dot(u, wlr_ref[...]).astype(BF16)
    z = _dot(lr, wal_ref[...]) + bal_ref[...]
    log_sig = jnp.minimum(z, 0.0) - jnp.log1p(jnp.exp(-jnp.abs(z)))
    g = log_sig * (1.0 / GLA_TAU)
    tm, n = g.shape
    for c0 in range(0, n, LANE):
        for r0 in range(0, tm, chunk):
            out_refs[-1][0, c0 // LANE, r0:r0 + chunk, :] = _scan_rows(g[r0:r0 + chunk, c0:c0 + LANE], c0 < n // 2)
    off = 0
    for (kind, width), o_ref in zip(segs, out_refs[:-1]):
        if kind == "glu":
            a = _dot(u, w_ref[:, off:off + width])
            b = _dot(u, w_ref[:, off + width:off + 2 * width])
            o_ref[0] = (a * jax.nn.sigmoid(b)).astype(o_ref.dtype)
            off += 2 * width
        else:
            p = _dot(u, w_ref[:, off:off + width])
            if kind == "sigmoid":
                p = jax.nn.sigmoid(p)
            if kind == "heads":
                hw = width // GLA_HEADS
                for h in range(GLA_HEADS):
                    o_ref[0, h] = p[:, h * hw:(h + 1) * hw].astype(o_ref.dtype)
            else:
                o_ref[0] = p.astype(o_ref.dtype)
            off += width


def _inproj_call(x, mod, mod_per_batch, g, w, wlr, wal, bal, segs, tm, chunk, name):
    bsz, s, d = x.shape
    assert tm % chunk == 0 and chunk % SUBLANE == 0
    mod_idx = (lambda b, j: (b, 0, 0)) if mod_per_batch else (lambda b, j: (0, 0, 0))
    n_dec = wal.shape[1]
    tok = lambda n: pl.BlockSpec((1, tm, n), lambda b, j: (b, j, 0))
    heads = lambda nh, w: pl.BlockSpec((1, nh, tm, w), lambda b, j: (b, 0, j, 0))
    out_shape, out_specs = [], []
    for kind, width in segs:
        if kind == "heads":
            out_shape.append(jax.ShapeDtypeStruct((bsz, GLA_HEADS, s, width // GLA_HEADS), BF16))
            out_specs.append(heads(GLA_HEADS, width // GLA_HEADS))
        else:
            out_shape.append(jax.ShapeDtypeStruct((bsz, s, width), BF16))
            out_specs.append(tok(width))
    out_shape.append(jax.ShapeDtypeStruct((bsz, n_dec // LANE, s, LANE), F32))
    out_specs.append(heads(n_dec // LANE, LANE))
    return pl.pallas_call(
        functools.partial(_inproj_kernel, segs=segs, chunk=chunk),
        grid=(bsz, s // tm),
        in_specs=[tok(d),
                  pl.BlockSpec((1, N_MOD, d), mod_idx),
                  _const_spec((1, d)),
                  _const_spec(w.shape),
                  _const_spec(wlr.shape),
                  _const_spec(wal.shape),
                  _const_spec((1, n_dec))],
        out_specs=out_specs,
        out_shape=out_shape,
        compiler_params=_params(("parallel", "parallel")),
        name=name,
    )(x, mod, g.reshape(1, d), w, wlr, wal, bal.reshape(1, n_dec))


def _chunk_rows(i, c):
    return pl.ds(i * c, c) if isinstance(i, int) else pl.ds(pl.multiple_of(i * c, c), c)


def _gla_kernel(*refs, n_operands, chunk, scale):
    operands, scratch = refs[:n_operands], refs[n_operands:]
    for h in range(operands[0].shape[0]):
        _gla_head(*[r.at[h] for r in operands], *[r.at[h] for r in scratch], chunk=chunk, scale=scale)


def _gla_head(q_ref, k_ref, v_ref, bf_ref, bb_ref, kc_ref, vc_ref, bfc_ref, bbc_ref, o_ref,
              qm_s, km_s, kl_s, att_s, qb_s, u_s, dec_s, st_s, *, chunk, scale):
    c = chunk
    dk = q_ref.shape[1]
    n = q_ref.shape[0] // c
    nc = kc_ref.shape[0] // c
    mid = c // 2
    row = lax.broadcasted_iota(jnp.int32, (c, c), 0)
    col = lax.broadcasted_iota(jnp.int32, (c, c), 1)
    lower = row >= col
    upper = row <= col

    uc, decc = [], []
    for i in range(nc):
        sl = slice(i * c, (i + 1) * c)
        bf, bb = bfc_ref[sl, :], bbc_ref[sl, :]
        blf, blb = bf[c - 1:c, :], bb[0:1, :]
        kcf = kc_ref[sl, :].astype(F32)
        kl = jnp.concatenate([(kcf * jnp.exp(blf - bf)).astype(BF16),
                              (kcf * jnp.exp(blb - bb)).astype(BF16)], axis=1)
        uc.append(_dot_t(vc_ref[sl, :], kl, 0, 0))
        decc.append((jnp.exp(blf), jnp.exp(blb)))
    stf0 = jnp.zeros((uc[0].shape[0], dk), F32)
    stb0 = stf0
    for i in range(nc):
        j = nc - 1 - i
        stf0 = stf0 * decc[i][0] + uc[i][:, :dk]
        stb0 = stb0 * decc[j][1] + uc[j][:, dk:]

    def factors(i):
        sl = _chunk_rows(i, c)
        bf, bb = bf_ref[sl, :], bb_ref[sl, :]
        blf, blb = bf[c - 1:c, :], bb[0:1, :]
        bmf, bmb = bf[mid:mid + 1, :], bb[mid:mid + 1, :]
        q = q_ref[sl, :].astype(F32) * scale
        k = k_ref[sl, :].astype(F32)
        qmf = q * jnp.exp(bf - bmf)
        qmb = q * jnp.exp(bb - bmb)
        kmf = k * jnp.exp(bmf - bf)
        kmb = k * jnp.exp(bmb - bb)
        qm_s[sl, :] = jnp.concatenate([qmf.astype(BF16), qmb.astype(BF16)], axis=1)
        km_s[sl, :] = jnp.concatenate([kmf.astype(BF16), kmb.astype(BF16)], axis=1)
        qb_s[sl, :] = jnp.concatenate([(qmf * jnp.exp(bmf)).astype(BF16),
                                       (qmb * jnp.exp(bmb)).astype(BF16)], axis=1)
        kl_s[sl, :] = jnp.concatenate([(kmf * jnp.exp(blf - bmf)).astype(BF16),
                                       (kmb * jnp.exp(blb - bmb)).astype(BF16)], axis=1)
        dec_s[i] = jnp.concatenate([jnp.exp(blf), jnp.exp(blb)], axis=1)

    def products(i):
        sl = _chunk_rows(i, c)
        qm, km = qm_s[sl, :], km_s[sl, :]
        att = (jnp.where(lower, _dot_t(qm[:, 0:dk], km[:, 0:dk], 1, 1), 0.0)
               + jnp.where(upper, _dot_t(qm[:, dk:2 * dk], km[:, dk:2 * dk], 1, 1), 0.0))
        att_s[sl, :] = att.astype(BF16)
        u_s[i] = _dot_t(v_ref[sl, :], kl_s[sl, :], 0, 0)

    def prep(i, carry):
        products(i - 1)
        factors(i)
        return carry

    factors(0)
    lax.fori_loop(1, n, prep, 0, unroll=15)
    products(n - 1)

    def scan(i, carry):
        stf, stb = carry
        j = n - 1 - i
        st_s[i, :, 0:dk] = stf.astype(BF16)
        st_s[j, :, dk:2 * dk] = stb.astype(BF16)
        stf = stf * dec_s[i][:, 0:dk] + u_s[i, :, 0:dk]
        stb = stb * dec_s[j][:, dk:2 * dk] + u_s[j, :, dk:2 * dk]
        return stf, stb

    lax.fori_loop(0, n, scan, (stf0, stb0))

    def emit(i, carry):
        sl = _chunk_rows(i, c)
        o_ref[sl, :] = _dot(att_s[sl, :], v_ref[sl, :]) + _dot_t(qb_s[sl, :], st_s[i], 1, 1)
        return carry

    lax.fori_loop(0, n, emit, 0, unroll=16)


def _gla_call(q, k, v, gd, kc, vc, gdc, chunk):
    bsz, nh, s, dk = q.shape
    dv = v.shape[3]
    sc = kc.shape[2]
    n = s // chunk
    hps = GLA_HEADS_PER_STEP
    assert s % chunk == 0 and sc % chunk == 0 and dk == LANE and gd.shape[1] == 2 * nh and nh % hps == 0
    hd = lambda rows, w, shift=0: pl.BlockSpec((None, hps, rows, w), lambda b, g: (b, g + shift, 0, 0))
    return pl.pallas_call(
        functools.partial(_gla_kernel, n_operands=10, chunk=chunk, scale=dk ** -0.5),
        grid=(bsz, nh // hps),
        in_specs=[hd(s, dk), hd(s, dk), hd(s, dv), hd(s, dk), hd(s, dk, nh // hps),
                  hd(sc, dk), hd(sc, dv), hd(sc, dk), hd(sc, dk, nh // hps)],
        out_specs=hd(s, dv),
        out_shape=jax.ShapeDtypeStruct((bsz, nh, s, dv), F32),
        scratch_shapes=[pltpu.VMEM((hps, s, 2 * dk), BF16),
                        pltpu.VMEM((hps, s, 2 * dk), BF16),
                        pltpu.VMEM((hps, s, 2 * dk), BF16),
                        pltpu.VMEM((hps, s, chunk), BF16),
                        pltpu.VMEM((hps, s, 2 * dk), BF16),
                        pltpu.VMEM((hps, n, dv, 2 * dk), F32),
                        pltpu.VMEM((hps, n, 1, 2 * dk), F32),
                        pltpu.VMEM((hps, n, dv, 2 * dk), BF16)],
        compiler_params=_params(("parallel", "parallel")),
        name="gla",
    )(q, k, v, gd, gd, kc, vc, gdc, gdc)


def _conv_geometry(width, halo):
    base = halo - width // 2
    n_a = -(-(width + base) // SUBLANE)
    return base, n_a, SUBLANE * (n_a - 1)


def _shift_selector(block, extra):
    span = block + extra
    rows = jnp.arange(SUBLANE * span)
    src = rows % span + rows // span
    return (jnp.arange(span + SUBLANE)[None, :] == src[:, None]).astype(BF16)


def _depthwise_conv(win_ref, sel_ref, sh_ref, dww_ref, out_ref, halo, row_block):
    tm, dc = out_ref.shape
    width = dww_ref.shape[0]
    base, n_a, extra = _conv_geometry(width, halo)
    span = sel_ref.shape[0] // SUBLANE
    block = span - extra
    for bi, r0 in enumerate(range(0, tm, block)):
        sh = sh_ref.at[bi % sh_ref.shape[0]]
        sh[...] = _dot(sel_ref[...], win_ref[r0:r0 + span + SUBLANE, :])
        for c0 in range(0, dc, LANE):
            w = dww_ref[:, c0:c0 + LANE]
            for q0 in range(0, block, row_block):
                acc = None
                for phase in range(SUBLANE):
                    for a in range(n_a):
                        d = SUBLANE * a + phase - base
                        if not 0 <= d < width:
                            continue
                        lo = phase * span + SUBLANE * a + q0
                        term = sh[lo:lo + row_block, c0:c0 + LANE] * w[d:d + 1, :]
                        acc = term if acc is None else acc + term
                out_ref[r0 + q0:r0 + q0 + row_block, c0:c0 + LANE] = acc


def _conv_branch(zc_ref, sel_ref, j, nj, tm, halo, win_ref, sh_ref, conv_ref, dww_ref, dwb_ref, lng_ref, lnb_ref):
    seq = zc_ref.shape[1]
    t0 = pl.multiple_of(j * tm, tm)
    prev_lo = pl.multiple_of(jnp.maximum(t0 - halo, 0), halo)
    next_lo = pl.multiple_of(jnp.minimum(t0 + tm, seq - halo), halo)
    prev = zc_ref[0, pl.ds(prev_lo, halo), :]
    nxt = zc_ref[0, pl.ds(next_lo, halo), :]
    win_ref[0:halo, :] = jnp.where(j > 0, prev, jnp.zeros_like(prev))
    win_ref[halo:halo + tm, :] = zc_ref[0, pl.ds(t0, tm), :]
    win_ref[halo + tm:, :] = jnp.where(j < nj - 1, nxt, jnp.zeros_like(nxt))

    _depthwise_conv(win_ref, sel_ref, sh_ref, dww_ref, conv_ref, halo, 8 * SUBLANE)
    z = conv_ref[...] + dwb_ref[...]
    mu = jnp.mean(z, axis=-1, keepdims=True)
    zc = z - mu
    var = jnp.mean(zc * zc, axis=-1, keepdims=True)
    z = zc * lax.rsqrt(var + EPS) * lng_ref[...] + lnb_ref[...]
    return _silu(z).astype(BF16)


def _merge_kernel(zc_ref, sel_ref, o_ref, og_ref, sga_ref, sgb_ref, x_ref, mod_ref, dww_ref, dwb_ref, lng_ref,
                  lnb_ref, gng_ref, wco_ref, wgo_ref, wo_ref, out_ref, win_ref, sh_ref, conv_ref, *, halo):
    tm = x_ref.shape[1]
    act = _conv_branch(zc_ref, sel_ref, pl.program_id(1), pl.num_programs(1), tm, halo, win_ref, sh_ref, conv_ref,
                       dww_ref, dwb_ref, lng_ref, lnb_ref)
    y_conv = _dot(act, wco_ref[...])

    parts = []
    for h in range(o_ref.shape[1]):
        oh = o_ref[0, h]
        parts.append(oh * lax.rsqrt(jnp.mean(oh * oh, axis=-1, keepdims=True) + EPS))
    on = jnp.concatenate(parts, axis=1) * gng_ref[...]
    y_gla = _dot((on * _silu(og_ref[0].astype(F32))).astype(BF16), wgo_ref[...])

    merged = sga_ref[0].astype(F32) * y_conv + sgb_ref[0].astype(F32) * y_gla
    mix = _dot(merged.astype(BF16), wo_ref[...])
    out_ref[0] = x_ref[0] + mod_ref[0, 5:6, :] * mix


def _merge_call(zc, o, og, sga, sgb, x1, mod, dw_w, dw_b, ln_g, ln_b, gn_g, wco, wgo, wo, tm):
    bsz, s, d = x1.shape
    dc = zc.shape[2]
    halo = 2 * SUBLANE
    block = LANE
    _, _, extra = _conv_geometry(dw_w.shape[0], halo)
    assert dw_w.shape[0] // 2 <= halo and s % tm == 0 and tm % block == 0 and extra + SUBLANE <= 2 * halo
    sel = _shift_selector(block, extra)
    tok = lambda n: pl.BlockSpec((1, tm, n), lambda b, j: (b, j, 0))
    row = lambda a: a.reshape(1, -1)
    return pl.pallas_call(
        functools.partial(_merge_kernel, halo=halo),
        grid=(bsz, s // tm),
        in_specs=[pl.BlockSpec((1, s, dc), lambda b, j: (b, 0, 0)),
                  _const_spec(sel.shape),
                  pl.BlockSpec((1,) + (o.shape[1], tm, o.shape[3]), lambda b, j: (b, 0, j, 0)),
                  tok(og.shape[2]), tok(d), tok(d), tok(d),
                  pl.BlockSpec((1, N_MOD, d), lambda b, j: (b, 0, 0)),
                  _const_spec(dw_w.shape), _const_spec((1, dc)), _const_spec((1, dc)), _const_spec((1, dc)),
                  _const_spec((1, og.shape[2])),
                  _const_spec(wco.shape), _const_spec(wgo.shape), _const_spec(wo.shape)],
        out_specs=tok(d),
        out_shape=jax.ShapeDtypeStruct((bsz, s, d), F32),
        scratch_shapes=[pltpu.VMEM((tm + 2 * halo, dc), zc.dtype),
                        pltpu.VMEM((2,) + (sel.shape[0], dc), F32),
                        pltpu.VMEM((tm, dc), F32)],
        compiler_params=_params(("parallel", "arbitrary")),
        name="merge",
    )(zc, sel, o, og, sga, sgb, x1, mod, dw_w, row(dw_b), row(ln_g), row(ln_b), row(gn_g), wco, wgo, wo)


def kernel(x, c, ctx, c_ctx, w_mod, b_mod, g_ffn1, w1_gu, w1_down, g_mix, w_in, dw_weight, dw_bias,
           conv_ln_g, conv_ln_b, w_conv_out, w_alpha_f, b_alpha_f, w_alpha_b, b_alpha_b, gla_norm_g,
           w_gla_out, w_out, g_ffn2, w2_gu, w2_down, g_final):
    depth = w_mod.shape[0]
    assert depth == 1, "context-side mixing of non-final layers is not implemented"
    bsz, seq, d = x.shape
    d_conv = dw_weight.shape[2]
    qk_w = w_alpha_f.shape[2]
    v_w = gla_norm_g.shape[1]
    r = GLA_LOWRANK

    pad_rows = (-(bsz + 1)) % SUBLANE
    c_all = jnp.concatenate([c, c_ctx[None, :], jnp.zeros((pad_rows, d), F32)], axis=0)
    mod = _mod_call(c_all, w_mod[0], b_mod[0])
    mod_x = mod[:bsz].reshape(bsz, N_MOD, d)
    mod_c = mod[bsz:bsz + 1].reshape(1, N_MOD, d)

    w1gu, w1d = w1_gu[0].astype(BF16), w1_down[0].astype(BF16)
    w2gu, w2d = w2_gu[0].astype(BF16), w2_down[0].astype(BF16)
    wco, wgo, wo = w_conv_out[0].astype(BF16), w_gla_out[0].astype(BF16), w_out[0].astype(BF16)
    wi = w_in[0]
    o_q = 2 * d_conv
    o_k, o_v = o_q + qk_w, o_q + 2 * qk_w
    o_og = o_v + v_w
    o_af = o_og + v_w
    o_ga = o_af + 2 * r
    w_x = jnp.concatenate([wi[:, :o_af], wi[:, o_ga:]], axis=1).astype(BF16)
    w_c = wi[:, o_k:o_og].astype(BF16)
    wlr = jnp.concatenate([wi[:, o_af:o_ga], jnp.zeros((d, LANE - 2 * r), F32)], axis=1).astype(BF16)
    wal = jnp.zeros((LANE, 2 * qk_w), F32)
    wal = wal.at[:r, :qk_w].set(w_alpha_f[0]).at[r:2 * r, qk_w:].set(w_alpha_b[0]).astype(BF16)
    bal = jnp.concatenate([b_alpha_f[0], b_alpha_b[0]])

    x1 = _ffn_call(x, mod_x, True, g_ffn1[0], w1gu, w1d, 0, 1024, name="ffn1_x")
    h1 = _ffn_call(ctx, mod_c, False, g_ffn1[0], w1gu, w1d, 0, ctx.shape[1], name="ffn1_ctx")

    segs_x = (("glu", d_conv), ("heads", qk_w), ("heads", qk_w), ("heads", v_w), ("plain", v_w),
              ("sigmoid", d), ("sigmoid", d))
    zc, q, k, v, og, sga, sgb, gd = _inproj_call(x1, mod_x, True, g_mix[0], w_x, wlr, wal, bal,
                                                 segs_x, 512, GLA_CHUNK, "inproj_x")
    segs_c = (("heads", qk_w), ("heads", v_w))
    kc, vc, gdc = _inproj_call(h1, mod_c, False, g_mix[0], w_c, wlr, wal, bal, segs_c,
                               ctx.shape[1], GLA_CHUNK, "inproj_ctx")

    o = _gla_call(q, k, v, gd, kc, vc, gdc, GLA_CHUNK)

    x2 = _merge_call(zc, o, og, sga, sgb, x1, mod_x, dw_weight[0], dw_bias[0], conv_ln_g[0],
                     conv_ln_b[0], gla_norm_g[0], wco, wgo, wo, 256)

    return _ffn_call(x2, mod_x, True, g_ffn2[0], w2gu, w2d, 6, 1024, "ffn2_x", g_final=g_final)
```
